```python
import jax, jax.numpy as jnp
from jax import lax
import numpy as np

D_MODEL = 4096
BATCH = 2
SEQ = 4096
DEPTH = 1
DEC_BATCH = 8
DEC_SEQ = 2048
PAST_LEN = 128

N_MEM = 256
HEAD_DIM = 64
ATTN_WIDTH = D_MODEL // 2
ATTN_HEADS = ATTN_WIDTH // HEAD_DIM
KV_HEADS = ATTN_HEADS // 8
GQA_GROUP = ATTN_HEADS // KV_HEADS
KV_WIDTH = KV_HEADS * HEAD_DIM
WINDOW = 128
BLOCK = 128
ROPE_DIM = HEAD_DIM // 4
ROPE_THETA = 500000.0
SGU_WIDTH = D_MODEL // 4
SGU_GROUPS = 4
SGU_GROUP_DIM = SGU_WIDTH // SGU_GROUPS
CHUNK = 128
XATTN_WIDTH = D_MODEL // 4
XATTN_HEADS = 4
XATTN_HEAD_DIM = XATTN_WIDTH // XATTN_HEADS
MIX_WIDTH = ATTN_WIDTH + SGU_WIDTH + XATTN_WIDTH
IN_SPLITS = tuple(int(i) for i in np.cumsum([ATTN_WIDTH, KV_WIDTH, KV_WIDTH, SGU_WIDTH, SGU_WIDTH]))
IN_WIDTH = ATTN_WIDTH + 2 * KV_WIDTH + 2 * SGU_WIDTH + XATTN_WIDTH
N_EXPERTS = 32
TOP_K = 4
D_EXPERT = D_MODEL // 8
SWIGLU_LIMIT = 7.0
SWIGLU_ALPHA = 1.702
EPS = 1e-5
NEG_INF = -1e30

kernel_name = 'hybrid_parallel_window_sgu_memory_moe_encoder'


def rmsnorm(x, g):
    xf = x.astype(jnp.float32)
    y = xf * lax.rsqrt(jnp.mean(xf * xf, axis=-1, keepdims=True) + EPS) * g.astype(jnp.float32)
    return y.astype(x.dtype)


def rope_tables(seq_len):
    inv_freq = ROPE_THETA ** (-jnp.arange(0, ROPE_DIM, 2, dtype=jnp.float32) / ROPE_DIM)
    ang = jnp.arange(seq_len, dtype=jnp.float32)[:, None] * inv_freq[None, :]
    return jnp.cos(ang), jnp.sin(ang)


def apply_partial_rope(x, cos, sin):
    xr, xp = x[..., :ROPE_DIM].astype(jnp.float32), x[..., ROPE_DIM:]
    x1, x2 = xr[..., :ROPE_DIM // 2], xr[..., ROPE_DIM // 2:]
    c, s = cos[None, :, None, :], sin[None, :, None, :]
    rot = jnp.concatenate([x1 * c - x2 * s, x2 * c + x1 * s], axis=-1).astype(x.dtype)
    return jnp.concatenate([rot, xp], axis=-1)


def windowed_gqa_with_sink(q, k, v, sink):
    B, S = q.shape[0], q.shape[1]
    nb = S // BLOCK
    qb = q.reshape(B, nb, BLOCK, KV_HEADS, GQA_GROUP, HEAD_DIM)

    def band(t):
        tp = jnp.pad(t, ((0, 0), (BLOCK, BLOCK), (0, 0), (0, 0)))
        tp = tp.reshape(B, nb + 2, BLOCK, KV_HEADS, HEAD_DIM)
        return jnp.concatenate([tp[:, :-2], tp[:, 1:-1], tp[:, 2:]], axis=2)

    kb, vb = band(k), band(v)
    s = jnp.einsum('bnqhgd,bnkhd->bnhgqk', qb, kb).astype(jnp.float32) * (HEAD_DIM ** -0.5)
    blk = jnp.arange(nb)[:, None, None] * BLOCK
    qpos = blk + jnp.arange(BLOCK)[None, :, None]
    kpos = blk - BLOCK + jnp.arange(3 * BLOCK)[None, None, :]
    valid = (jnp.abs(kpos - qpos) <= WINDOW) & (kpos >= 0) & (kpos < S)
    s = jnp.where(valid[None, :, None, None], s, NEG_INF)
    sk = jnp.broadcast_to(sink.astype(jnp.float32).reshape(KV_HEADS, GQA_GROUP)[None, None, :, :, None, None],
                          s.shape[:-1] + (1,))
    p = jax.nn.softmax(jnp.concatenate([s, sk], axis=-1), axis=-1)[..., :-1]
    o = jnp.einsum('bnhgqk,bnkhd->bnqhgd', p.astype(v.dtype), vb)
    return o.reshape(B, S, ATTN_WIDTH)


def chunked_spatial_gating(u, v, g_sgu, w_s, b_s):
    B, S = u.shape[0], u.shape[1]
    u = jax.nn.gelu(u)
    vg = jax.nn.gelu(v).reshape(B, S // CHUNK, CHUNK, SGU_GROUPS, SGU_GROUP_DIM)
    vg = rmsnorm(vg, g_sgu.reshape(SGU_GROUPS, SGU_GROUP_DIM))
    mixed = jnp.einsum('hpq,bcqhd->bcphd', w_s, vg) + b_s.T[None, None, :, :, None]
    return u * mixed.reshape(B, S, SGU_WIDTH)


def memory_cross_attention(q, mem, g_mem, w_mem_kv):
    B, S = q.shape[0], q.shape[1]
    M = mem.shape[1]
    kv = rmsnorm(mem, g_mem) @ w_mem_kv
    k, v = jnp.split(kv, 2, axis=-1)
    qh = q.reshape(B, S, XATTN_HEADS, XATTN_HEAD_DIM)
    kh = k.reshape(B, M, XATTN_HEADS, XATTN_HEAD_DIM)
    vh = v.reshape(B, M, XATTN_HEADS, XATTN_HEAD_DIM)
    s = jnp.einsum('bshd,bmhd->bhsm', qh, kh).astype(jnp.float32) * (XATTN_HEAD_DIM ** -0.5)
    p = jax.nn.softmax(s, axis=-1).astype(v.dtype)
    return jnp.einsum('bhsm,bmhd->bshd', p, vh).reshape(B, S, XATTN_WIDTH)


def parallel_mixer(xn, mem, cos, sin, w_in, attn_sink, g_sgu, w_s, b_s, g_mem, w_mem_kv, g_group, w_out):
    B, S = xn.shape[0], xn.shape[1]
    proj = xn @ w_in
    q, k, v, u, vs, xq = jnp.split(proj, IN_SPLITS, axis=-1)
    q = apply_partial_rope(q.reshape(B, S, ATTN_HEADS, HEAD_DIM), cos, sin)
    k = apply_partial_rope(k.reshape(B, S, KV_HEADS, HEAD_DIM), cos, sin)
    v = v.reshape(B, S, KV_HEADS, HEAD_DIM)
    o_attn = windowed_gqa_with_sink(q, k, v, attn_sink)
    o_sgu = chunked_spatial_gating(u, vs, g_sgu, w_s, b_s)
    o_x = memory_cross_attention(xq, mem, g_mem, w_mem_kv)
    a1, a2 = ATTN_WIDTH, ATTN_WIDTH + SGU_WIDTH
    merged = jnp.concatenate([rmsnorm(o_attn, g_group[:a1]),
                              rmsnorm(o_sgu, g_group[a1:a2]),
                              rmsnorm(o_x, g_group[a2:])], axis=-1)
    return merged @ w_out


def moe_clamped_swiglu(xn, w_router, b_router, w_gu, b_gu, w_down, b_down):
    B, S, D = xn.shape
    t = xn.reshape(B * S, D)
    logits = (t @ w_router).astype(jnp.float32) + b_router.astype(jnp.float32)
    top_vals, top_idx = lax.top_k(logits, TOP_K)
    gates = jax.nn.softmax(top_vals, axis=-1)
    combine = jnp.einsum('tk,tke->te', gates, jax.nn.one_hot(top_idx, N_EXPERTS, dtype=jnp.float32))
    gu = jnp.einsum('td,edf->tef', t, w_gu) + b_gu[None]
    gate, up = jnp.split(gu, 2, axis=-1)
    gate = jnp.minimum(gate, SWIGLU_LIMIT)
    up = jnp.clip(up, -SWIGLU_LIMIT, SWIGLU_LIMIT)
    h = (up + 1) * (gate * jax.nn.sigmoid(SWIGLU_ALPHA * gate)) * combine[:, :, None].astype(gu.dtype)
    out = jnp.einsum('tef,efd->td', h, w_down) + combine.astype(b_down.dtype) @ b_down
    return out.astype(xn.dtype).reshape(B, S, D)


def run_trunk(x, mem, g_mix, w_in, attn_sink, g_sgu, w_spatial, b_spatial, g_mem, w_mem_kv, g_group, w_out,
              g_ffn, w_router, b_router, w_gate_up, b_gate_up, w_down, b_down, g_final):
    cos, sin = rope_tables(x.shape[1])
    h = x
    for l in range(DEPTH):
        h = h + parallel_mixer(rmsnorm(h, g_mix[l]), mem, cos, sin, w_in[l], attn_sink[l], g_sgu[l],
                               w_spatial[l], b_spatial[l], g_mem[l], w_mem_kv[l], g_group[l], w_out[l])
        h = h + moe_clamped_swiglu(rmsnorm(h, g_ffn[l]), w_router[l], b_router[l], w_gate_up[l],
                                   b_gate_up[l], w_down[l], b_down[l])
    return rmsnorm(h, g_final)


def setup_inputs(seed: int = 0) -> dict:
    key = jax.random.key(seed)
    ks = jax.random.split(key, 24)
    f32 = jnp.float32

    def nrm(k, shape, scale):
        return jax.random.normal(k, shape, f32) * scale

    def gain(k, shape):
        return 1.0 + 0.05 * jax.random.normal(k, shape, f32)

    return {
        'x_prompt': nrm(ks[0], (BATCH, SEQ, D_MODEL), 1.0),
        'x_sample': nrm(ks[1], (DEC_BATCH, DEC_SEQ, D_MODEL), 1.0),
        'mem_prompt': nrm(ks[2], (BATCH, N_MEM, D_MODEL), 1.0),
        'mem_sample': nrm(ks[3], (DEC_BATCH, N_MEM, D_MODEL), 1.0),
        'g_mix': gain(ks[4], (DEPTH, D_MODEL)),
        'w_in': nrm(ks[5], (DEPTH, D_MODEL, IN_WIDTH), D_MODEL ** -0.5),
        'attn_sink': nrm(ks[6], (DEPTH, ATTN_HEADS), 1.0),
        'g_sgu': gain(ks[7], (DEPTH, SGU_WIDTH)),
        'w_spatial': nrm(ks[8], (DEPTH, SGU_GROUPS, CHUNK, CHUNK), CHUNK ** -0.5),
        'b_spatial': 1.0 + nrm(ks[9], (DEPTH, SGU_GROUPS, CHUNK), 0.1),
        'g_mem': gain(ks[10], (DEPTH, D_MODEL)),
        'w_mem_kv': nrm(ks[11], (DEPTH, D_MODEL, 2 * XATTN_WIDTH), D_MODEL ** -0.5),
        'g_group': gain(ks[12], (DEPTH, MIX_WIDTH)),
        'w_out': nrm(ks[13], (DEPTH, MIX_WIDTH, D_MODEL), MIX_WIDTH ** -0.5),
        'g_ffn': gain(ks[14], (DEPTH, D_MODEL)),
        'w_router': nrm(ks[15], (DEPTH, D_MODEL, N_EXPERTS), D_MODEL ** -0.5),
        'b_router': nrm(ks[16], (DEPTH, N_EXPERTS), 0.01),
        'w_gate_up': nrm(ks[17], (DEPTH, N_EXPERTS, D_MODEL, 2 * D_EXPERT), D_MODEL ** -0.5),
        'b_gate_up': nrm(ks[18], (DEPTH, N_EXPERTS, 2 * D_EXPERT), 0.02),
        'w_down': nrm(ks[19], (DEPTH, N_EXPERTS, D_EXPERT, D_MODEL), D_EXPERT ** -0.5),
        'b_down': nrm(ks[20], (DEPTH, N_EXPERTS, D_MODEL), 0.02),
        'g_final': gain(ks[21], (D_MODEL,)),
    }


def reference(x_prompt, x_sample, mem_prompt, mem_sample, g_mix, w_in, attn_sink, g_sgu, w_spatial, b_spatial,
              g_mem, w_mem_kv, g_group, w_out, g_ffn, w_router, b_router, w_gate_up, b_gate_up, w_down, b_down,
              g_final):
    y_prompt = run_trunk(x_prompt, mem_prompt, g_mix, w_in, attn_sink, g_sgu, w_spatial, b_spatial, g_mem,
                         w_mem_kv, g_group, w_out, g_ffn, w_router, b_router, w_gate_up, b_gate_up, w_down,
                         b_down, g_final)
    y_sample = run_trunk(x_sample, mem_sample, g_mix, w_in, attn_sink, g_sgu, w_spatial, b_spatial, g_mem,
                         w_mem_kv, g_group, w_out, g_ffn, w_router, b_router, w_gate_up, b_gate_up, w_down,
                         b_down, g_final)
    return (y_prompt, y_sample)
```

```python
import functools

import jax
import jax.numpy as jnp
import numpy as np
from jax import lax
from jax.experimental import pallas as pl
from jax.experimental.pallas import tpu as pltpu

D_MODEL = 4096
N_MEM = 256
HEAD_DIM = 64
ATTN_WIDTH = 2048
ATTN_HEADS = 32
KV_HEADS = 4
KV_WIDTH = 256
WINDOW = 128
BLOCK = 128
ROPE_DIM = 16
ROPE_THETA = 500000.0
SGU_WIDTH = 1024
SGU_GROUPS = 4
SGU_GROUP_DIM = 256
CHUNK = 128
XATTN_WIDTH = 1024
XATTN_HEADS = 4
XATTN_HEAD_DIM = 256
IN_WIDTH = 5632
N_EXPERTS = 32
TOP_K = 4
D_EXPERT = 512
SWIGLU_LIMIT = 7.0
SWIGLU_ALPHA = 1.702
EPS = 1e-5
NEG_INF = -1e30

LANES = 128
COL_Q, COL_U, COL_VS, COL_XQ, COL_K, COL_V = 0, 2048, 3072, 4096, 5120, 5376

F32 = jnp.float32
BF16 = jnp.bfloat16
MIB = 1024 * 1024


def _params(semantics, vmem_mib):
    return pltpu.CompilerParams(dimension_semantics=semantics, vmem_limit_bytes=vmem_mib * MIB)


def _tile(n, pref):
    t = min(n, pref)
    while n % t or t % LANES:
        t -= LANES
    assert t > 0
    return t


def _rope_group(a, c, s1, s2):
    return a * c + pltpu.roll(a, LANES - ROPE_DIM // 2, 1) * s1 + pltpu.roll(a, ROPE_DIM // 2, 1) * s2


def _norm_proj_kernel(*refs, rope, tn):
    if rope:
        x_ref, g_ref, w_ref, c_ref, s1_ref, s2_ref, o_ref, xn_ref = refs
    else:
        x_ref, g_ref, w_ref, o_ref, xn_ref = refs
    j = pl.program_id(1)

    @pl.when(j == 0)
    def _():
        x = x_ref[...]
        ms = jnp.mean(x * x, axis=-1, keepdims=True)
        xn_ref[...] = (x * lax.rsqrt(ms + EPS) * g_ref[...]).astype(BF16)

    acc = jnp.dot(xn_ref[...], w_ref[...], preferred_element_type=F32)
    if not rope:
        o_ref[...] = acc.astype(o_ref.dtype)
        return

    ngroups = tn // LANES
    q_tiles = ATTN_WIDTH // tn
    k_tile = COL_K // tn
    k_groups = KV_WIDTH // LANES

    def store(n_rope):
        c, s1, s2 = c_ref[...], s1_ref[...], s2_ref[...]
        for gidx in range(ngroups):
            a = acc[:, gidx * LANES:(gidx + 1) * LANES]
            if gidx < n_rope:
                a = _rope_group(a, c, s1, s2)
            o_ref[:, gidx * LANES:(gidx + 1) * LANES] = a.astype(o_ref.dtype)

    @pl.when(j < q_tiles)
    def _():
        store(ngroups)

    @pl.when(j == k_tile)
    def _():
        store(k_groups)

    @pl.when(jnp.logical_and(j >= q_tiles, j != k_tile))
    def _():
        store(0)


def _norm_proj(x, g, w, rope_tabs=None, seq=None, *, tm_pref=512, tn=512):
    t, d = x.shape
    n = w.shape[1]
    tm = _tile(t if seq is None else seq, tm_pref)
    rope = rope_tabs is not None
    in_specs = [
        pl.BlockSpec((tm, d), lambda i, j: (i, 0)),
        pl.BlockSpec((1, d), lambda i, j: (0, 0)),
        pl.BlockSpec((d, tn), lambda i, j: (0, j)),
    ]
    args = [x, g.reshape(1, d), w]
    if rope:
        assert COL_K % tn == 0 and ATTN_WIDTH % tn == 0 and tn >= KV_WIDTH
        sblocks = seq // tm
        tab_spec = pl.BlockSpec((tm, LANES), lambda i, j: (i % sblocks, 0))
        in_specs += [tab_spec, tab_spec, tab_spec]
        args += list(rope_tabs)
    return pl.pallas_call(
        functools.partial(_norm_proj_kernel, rope=rope, tn=tn),
        out_shape=jax.ShapeDtypeStruct((t, n), BF16),
        grid=(t // tm, n // tn),
        in_specs=in_specs,
        out_specs=pl.BlockSpec((tm, tn), lambda i, j: (i, j)),
        scratch_shapes=[pltpu.VMEM((tm, d), BF16)],
        compiler_params=_params(("parallel", "arbitrary"), 48),
        name="norm_proj_rope" if rope else "norm_proj",
    )(*args)


def _rope_tables(seq):
    half = ROPE_DIM // 2
    inv_freq = ROPE_THETA ** (-jnp.arange(0, ROPE_DIM, 2, dtype=F32) / ROPE_DIM)
    ang = jnp.arange(seq, dtype=F32)[:, None] * inv_freq[None, :]
    cos, sin = jnp.cos(ang), jnp.sin(ang)
    dim = np.arange(LANES) % HEAD_DIM
    sel = dim % half
    cos_l, sin_l = cos[:, sel], sin[:, sel]
    c = jnp.where(dim < ROPE_DIM, cos_l, 1.0)
    s1 = jnp.where(dim < half, -sin_l, 0.0)
    s2 = jnp.where((dim >= half) & (dim < ROPE_DIM), sin_l, 0.0)
    return c.astype(F32), s1.astype(F32), s2.astype(F32)


def _window_attn_kernel(sink_ref, q_ref, kp_ref, kc_ref, kn_ref, vp_ref, vc_ref, vn_ref, gg_ref,
                        o_ref, obuf_ref, *, nb):
    n = pl.program_id(1)
    band = 3 * BLOCK
    qi = lax.broadcasted_iota(jnp.int32, (BLOCK, band), 0)
    ki = lax.broadcasted_iota(jnp.int32, (BLOCK, band), 1)
    rel = ki - BLOCK - qi
    lo = jnp.where(n == 0, BLOCK, 0)
    hi = jnp.where(n == nb - 1, 2 * BLOCK, band)
    valid = (jnp.abs(rel) <= WINDOW) & (ki >= lo) & (ki < hi)
    lane_k = lax.broadcasted_iota(jnp.int32, (band, LANES), 1)
    left_k = lane_k < HEAD_DIM
    lane_q = lax.broadcasted_iota(jnp.int32, (BLOCK, LANES), 1)
    left_q = lane_q < HEAD_DIM
    ones_l = jnp.where(left_k, 1.0, 0.0).astype(F32)
    ones_r = 1.0 - ones_l
    scale = HEAD_DIM ** -0.5
    ssq = jnp.zeros((BLOCK, 1), F32)

    for h in range(KV_HEADS):
        slab = (h // 2) * LANES
        k3 = jnp.concatenate([r[:, slab:slab + LANES] for r in (kp_ref, kc_ref, kn_ref)], axis=0).astype(F32)
        v3 = jnp.concatenate([r[:, slab:slab + LANES] for r in (vp_ref, vc_ref, vn_ref)], axis=0).astype(F32)
        k3r = pltpu.roll(k3, HEAD_DIM, 1)
        v3r = pltpu.roll(v3, HEAD_DIM, 1)
        if h % 2 == 0:
            ka, kb, va, vb = k3, k3r, v3, v3r
        else:
            ka, kb, va, vb = k3r, k3, v3r, v3
        kbd = jnp.concatenate([jnp.where(left_k, ka, 0.0), jnp.where(left_k, 0.0, kb)], axis=0).astype(BF16)
        vbd = jnp.concatenate(
            [jnp.concatenate([jnp.where(left_k, va, 0.0), ones_l], axis=1),
             jnp.concatenate([jnp.where(left_k, 0.0, vb), ones_r], axis=1)], axis=0).astype(BF16)
        for p in range(ATTN_HEADS // KV_HEADS // 2):
            g = h * 4 + p
            q2 = q_ref[:, g * LANES:(g + 1) * LANES]
            s2 = lax.dot_general(q2, kbd, (((1,), (1,)), ((), ())), preferred_element_type=F32)
            sa = jnp.where(valid, s2[:, :band] * scale, NEG_INF)
            sb = jnp.where(valid, s2[:, band:] * scale, NEG_INF)
            sink_a = sink_ref[2 * g]
            sink_b = sink_ref[2 * g + 1]
            ma = jnp.maximum(jnp.max(sa, axis=-1, keepdims=True), sink_a)
            mb = jnp.maximum(jnp.max(sb, axis=-1, keepdims=True), sink_b)
            p2 = jnp.concatenate([jnp.exp(sa - ma), jnp.exp(sb - mb)], axis=1).astype(BF16)
            o2 = jnp.dot(p2, vbd, preferred_element_type=F32)
            den = o2[:, LANES:] + jnp.where(left_q, jnp.exp(sink_a - ma), jnp.exp(sink_b - mb))
            out = o2[:, :LANES] / den
            ssq = ssq + jnp.sum(out * out, axis=-1, keepdims=True)
            obuf_ref[:, g * LANES:(g + 1) * LANES] = out

    r = lax.rsqrt(ssq * (1.0 / ATTN_WIDTH) + EPS)
    o_ref[...] = (obuf_ref[...] * r * gg_ref[...]).astype(o_ref.dtype)


def _window_attn(proj3, sink, gg):
    b, s, _ = proj3.shape
    nb = s // BLOCK
    kcol, vcol = COL_K // KV_WIDTH, COL_V // KV_WIDTH

    def band_spec(col, shift):
        return pl.BlockSpec((None, BLOCK, KV_WIDTH),
                            lambda bi, n: (bi, jnp.clip(n + shift, 0, nb - 1), col))

    return pl.pallas_call(
        functools.partial(_window_attn_kernel, nb=nb),
        out_shape=jax.ShapeDtypeStruct((b, s, ATTN_WIDTH), BF16),
        grid=(b, nb),
        in_specs=[
            pl.BlockSpec(memory_space=pltpu.SMEM),
            pl.BlockSpec((None, BLOCK, ATTN_WIDTH), lambda bi, n: (bi, n, 0)),
            band_spec(kcol, -1), band_spec(kcol, 0), band_spec(kcol, 1),
            band_spec(vcol, -1), band_spec(vcol, 0), band_spec(vcol, 1),
            pl.BlockSpec((1, ATTN_WIDTH), lambda bi, n: (0, 0)),
        ],
        out_specs=pl.BlockSpec((None, BLOCK, ATTN_WIDTH), lambda bi, n: (bi, n, 0)),
        scratch_shapes=[pltpu.VMEM((BLOCK, ATTN_WIDTH), F32)],
        compiler_params=_params(("parallel", "parallel"), 32),
        name="window_attn",
    )(sink, proj3, proj3, proj3, proj3, proj3, proj3, proj3, gg.reshape(1, ATTN_WIDTH))


def _sgu_kernel(u_ref, v_ref, ws_ref, bs_ref, gs_ref, gg_ref, o_ref, *, chunks):
    for c in range(chunks):
        rows = slice(c * CHUNK, (c + 1) * CHUNK)
        ug = jax.nn.gelu(u_ref[rows, :].astype(F32))
        vg = jax.nn.gelu(v_ref[rows, :].astype(F32))
        mixed = []
        for h in range(SGU_GROUPS):
            cols = slice(h * SGU_GROUP_DIM, (h + 1) * SGU_GROUP_DIM)
            vh = vg[:, cols]
            r = lax.rsqrt(jnp.mean(vh * vh, axis=-1, keepdims=True) + EPS)
            vn = (vh * r * gs_ref[:, cols]).astype(BF16)
            mixed.append(jnp.dot(ws_ref[h], vn, preferred_element_type=F32))
        o = ug * (jnp.concatenate(mixed, axis=1) + bs_ref[...])
        r = lax.rsqrt(jnp.mean(o * o, axis=-1, keepdims=True) + EPS)
        o_ref[rows, :] = (o * r * gg_ref[...]).astype(o_ref.dtype)


def _sgu(proj3, ws, bs_full, gs, gg, *, rows_pref=512):
    b, s, _ = proj3.shape
    tr = _tile(s, rows_pref)
    const = lambda bi, n: (0, 0)
    return pl.pallas_call(
        functools.partial(_sgu_kernel, chunks=tr // CHUNK),
        out_shape=jax.ShapeDtypeStruct((b, s, SGU_WIDTH), BF16),
        grid=(b, s // tr),
        in_specs=[
            pl.BlockSpec((None, tr, SGU_WIDTH), lambda bi, n: (bi, n, COL_U // SGU_WIDTH)),
            pl.BlockSpec((None, tr, SGU_WIDTH), lambda bi, n: (bi, n, COL_VS // SGU_WIDTH)),
            pl.BlockSpec((SGU_GROUPS, CHUNK, CHUNK), lambda bi, n: (0, 0, 0)),
            pl.BlockSpec((CHUNK, SGU_WIDTH), const),
            pl.BlockSpec((1, SGU_WIDTH), const),
            pl.BlockSpec((1, SGU_WIDTH), const),
        ],
        out_specs=pl.BlockSpec((None, tr, SGU_WIDTH), lambda bi, n: (bi, n, 0)),
        compiler_params=_params(("parallel", "parallel"), 32),
        name="sgu",
    )(proj3, proj3, ws, bs_full, gs.reshape(1, SGU_WIDTH), gg.reshape(1, SGU_WIDTH))


def _mem_xattn_kernel(q_ref, kv_ref, gg_ref, o_ref):
    scale = XATTN_HEAD_DIM ** -0.5
    outs = []
    ssq = None
    for h in range(XATTN_HEADS):
        cols = slice(h * XATTN_HEAD_DIM, (h + 1) * XATTN_HEAD_DIM)
        vcols = slice(XATTN_WIDTH + h * XATTN_HEAD_DIM, XATTN_WIDTH + (h + 1) * XATTN_HEAD_DIM)
        s = lax.dot_general(q_ref[:, cols], kv_ref[:, cols], (((1,), (1,)), ((), ())),
                            preferred_element_type=F32) * scale
        m = jnp.max(s, axis=-1, keepdims=True)
        p = jnp.exp(s - m)
        l = jnp.sum(p, axis=-1, keepdims=True)
        o = jnp.dot(p.astype(BF16), kv_ref[:, vcols], preferred_element_type=F32) / l
        sq = jnp.sum(o * o, axis=-1, keepdims=True)
        ssq = sq if ssq is None else ssq + sq
        outs.append(o)
    r = lax.rsqrt(ssq * (1.0 / XATTN_WIDTH) + EPS)
    for h in range(XATTN_HEADS):
        cols = slice(h * XATTN_HEAD_DIM, (h + 1) * XATTN_HEAD_DIM)
        o_ref[:, cols] = (outs[h] * r * gg_ref[:, cols]).astype(o_ref.dtype)


def _mem_xattn(proj3, kv3, gg, *, rows_pref=512):
    b, s, _ = proj3.shape
    tq = _tile(s, rows_pref)
    return pl.pallas_call(
        _mem_xattn_kernel,
        out_shape=jax.ShapeDtypeStruct((b, s, XATTN_WIDTH), BF16),
        grid=(b, s // tq),
        in_specs=[
            pl.BlockSpec((None, tq, XATTN_WIDTH), lambda bi, n: (bi, n, COL_XQ // XATTN_WIDTH)),
            pl.BlockSpec((None, N_MEM, 2 * XATTN_WIDTH), lambda bi, n: (bi, 0, 0)),
            pl.BlockSpec((1, XATTN_WIDTH), lambda bi, n: (0, 0)),
        ],
        out_specs=pl.BlockSpec((None, tq, XATTN_WIDTH), lambda bi, n: (bi, n, 0)),
        compiler_params=_params(("parallel", "parallel"), 32),
        name="mem_xattn",
    )(proj3, kv3, gg.reshape(1, XATTN_WIDTH))


def _out_proj_kernel(oa_ref, os_ref, ox_ref, w_ref, x_ref, h_ref):
    a1, a2 = ATTN_WIDTH, ATTN_WIDTH + SGU_WIDTH
    acc = jnp.dot(oa_ref[...], w_ref[:a1, :], preferred_element_type=F32)
    acc += jnp.dot(os_ref[...], w_ref[a1:a2, :], preferred_element_type=F32)
    acc += jnp.dot(ox_ref[...], w_ref[a2:, :], preferred_element_type=F32)
    h_ref[...] = x_ref[...] + acc


def _out_proj(oa, osg, ox, w, x, *, tm_pref=512, tn=1024):
    t, d = x.shape
    tm = _tile(t, tm_pref)
    return pl.pallas_call(
        _out_proj_kernel,
        out_shape=jax.ShapeDtypeStruct((t, d), F32),
        grid=(t // tm, d // tn),
        in_specs=[
            pl.BlockSpec((tm, ATTN_WIDTH), lambda i, j: (i, 0)),
            pl.BlockSpec((tm, SGU_WIDTH), lambda i, j: (i, 0)),
            pl.BlockSpec((tm, XATTN_WIDTH), lambda i, j: (i, 0)),
            pl.BlockSpec((d, tn), lambda i, j: (0, j)),
            pl.BlockSpec((tm, tn), lambda i, j: (i, j)),
        ],
        out_specs=pl.BlockSpec((tm, tn), lambda i, j: (i, j)),
        compiler_params=_params(("parallel", "arbitrary"), 48),
        name="out_proj",
    )(oa, osg, ox, w, x)


def _router_kernel(h_ref, g_ref, wr_ref, br_ref, xn_ref, idx_ref, gate_ref):
    x = h_ref[...]
    ms = jnp.mean(x * x, axis=-1, keepdims=True)
    xn = x * lax.rsqrt(ms + EPS) * g_ref[...]
    xn_ref[...] = xn
    logits = jnp.dot(xn.astype(BF16), wr_ref[...], preferred_element_type=F32) + br_ref[...]
    tm = logits.shape[0]
    lane_e = lax.broadcasted_iota(jnp.int32, (tm, N_EXPERTS), 1)
    lane_o = lax.broadcasted_iota(jnp.int32, (tm, LANES), 1)
    vals, idxs = [], []
    l = logits
    for _ in range(TOP_K):
        m = jnp.max(l, axis=-1, keepdims=True)
        i = jnp.min(jnp.where(l == m, lane_e, N_EXPERTS), axis=-1, keepdims=True)
        vals.append(m)
        idxs.append(i)
        l = jnp.where(lane_e == i, -jnp.inf, l)
    es = [jnp.exp(v - vals[0]) for v in vals]
    den = es[0] + es[1] + es[2] + es[3]
    idx_out = jnp.zeros((tm, LANES), jnp.int32)
    gate_out = jnp.zeros((tm, LANES), F32)
    for k in range(TOP_K):
        idx_out = jnp.where(lane_o == k, idxs[k], idx_out)
        gate_out = jnp.where(lane_o == k, es[k] / den, gate_out)
    idx_ref[...] = idx_out
    gate_ref[...] = gate_out


def _router(h, g, wr, br, *, tm_pref=256):
    t, d = h.shape
    tm = _tile(t, tm_pref)
    return pl.pallas_call(
        _router_kernel,
        out_shape=(jax.ShapeDtypeStruct((t, d), F32),
                   jax.ShapeDtypeStruct((t, LANES), jnp.int32),
                   jax.ShapeDtypeStruct((t, LANES), F32)),
        grid=(t // tm,),
        in_specs=[
            pl.BlockSpec((tm, d), lambda i: (i, 0)),
            pl.BlockSpec((1, d), lambda i: (0, 0)),
            pl.BlockSpec((d, N_EXPERTS), lambda i: (0, 0)),
            pl.BlockSpec((1, N_EXPERTS), lambda i: (0, 0)),
        ],
        out_specs=(pl.BlockSpec((tm, d), lambda i: (i, 0)),
                   pl.BlockSpec((tm, LANES), lambda i: (i, 0)),
                   pl.BlockSpec((tm, LANES), lambda i: (i, 0))),
        compiler_params=_params(("parallel",), 48),
        name="router",
    )(h, g.reshape(1, d), wr, br.reshape(1, N_EXPERTS))


def _moe_kernel(nt_ref, te_ref, tok_ref, tokn_ref, dst_ref, xn_hbm, wgu_ref, bgu_ref, wd_ref, bd_ref,
                y_hbm, xbuf, ybuf, gsem, ssem, *, tm):
    del te_ref
    i = pl.program_id(0)
    last = pl.num_programs(0) - 1
    nt = nt_ref[0]
    slot = i % 2

    def gather_copy(idx_ref, r, s):
        return pltpu.make_async_copy(xn_hbm.at[pl.ds(idx_ref[0, 0, r], 1), :],
                                     xbuf.at[s, pl.ds(r, 1), :], gsem.at[s])

    def scatter_copy(r, s):
        return pltpu.make_async_copy(ybuf.at[s, pl.ds(r, 1), :],
                                     y_hbm.at[pl.ds(dst_ref[0, 0, r], 1), :], ssem.at[s])

    def for_rows(fn):
        def body(r, c):
            fn(r)
            return c
        lax.fori_loop(0, tm, body, 0, unroll=8)

    def scatter_wait(s):
        for_rows(lambda r: pltpu.make_async_copy(ybuf.at[s, pl.ds(r, 1), :],
                                                 y_hbm.at[pl.ds(0, 1), :], ssem.at[s]).wait())

    @pl.when(jnp.logical_and(i == 0, nt > 0))
    def _():
        for_rows(lambda r: gather_copy(tok_ref, r, 0).start())

    @pl.when(i + 1 < nt)
    def _():
        for_rows(lambda r: gather_copy(tokn_ref, r, 1 - slot).start())

    @pl.when(i < nt)
    def _():
        for_rows(lambda r: gather_copy(tok_ref, r, slot).wait())

        @pl.when(i >= 2)
        def _():
            scatter_wait(slot)

        x = xbuf[slot].astype(BF16)
        gu = jnp.dot(x, wgu_ref[...], preferred_element_type=F32) + bgu_ref[...]
        gate = jnp.minimum(gu[:, :D_EXPERT], SWIGLU_LIMIT)
        up = jnp.clip(gu[:, D_EXPERT:], -SWIGLU_LIMIT, SWIGLU_LIMIT)
        hid = (up + 1.0) * (gate * jax.nn.sigmoid(SWIGLU_ALPHA * gate))
        ybuf[slot] = jnp.dot(hid.astype(BF16), wd_ref[...], preferred_element_type=F32) + bd_ref[...]
        for_rows(lambda r: scatter_copy(r, slot).start())

    @pl.when(i == last)
    def _():
        @pl.when(nt >= 1)
        def _():
            scatter_wait((nt - 1) % 2)

        @pl.when(nt >= 2)
        def _():
            scatter_wait(nt % 2)


def _moe(xn, nt, te, tok, dst, wgu, bgu, wd, bd, *, tm, y_rows):
    ntmax = te.shape[0]
    d = xn.shape[1]
    row_spec = lambda shift: pl.BlockSpec(
        (1, 1, tm), lambda i, nt_, te_: (jnp.minimum(i + shift, ntmax - 1), 0, 0), memory_space=pltpu.SMEM)
    grid_spec = pltpu.PrefetchScalarGridSpec(
        num_scalar_prefetch=2,
        grid=(ntmax,),
        in_specs=[
            row_spec(0), row_spec(1), row_spec(0),
            pl.BlockSpec(memory_space=pl.ANY),
            pl.BlockSpec((None, d, 2 * D_EXPERT), lambda i, nt_, te_: (te_[i], 0, 0)),
            pl.BlockSpec((None, 1, 2 * D_EXPERT), lambda i, nt_, te_: (te_[i], 0, 0)),
            pl.BlockSpec((None, D_EXPERT, d), lambda i, nt_, te_: (te_[i], 0, 0)),
            pl.BlockSpec((None, 1, d), lambda i, nt_, te_: (te_[i], 0, 0)),
        ],
        out_specs=pl.BlockSpec(memory_space=pl.ANY),
        scratch_shapes=[
            pltpu.VMEM((2, tm, d), F32),
            pltpu.VMEM((2, tm, d), F32),
            pltpu.SemaphoreType.DMA((2,)),
            pltpu.SemaphoreType.DMA((2,)),
        ],
    )
    return pl.pallas_call(
        functools.partial(_moe_kernel, tm=tm),
        out_shape=jax.ShapeDtypeStruct((y_rows, d), F32),
        grid_spec=grid_spec,
        compiler_params=_params(("arbitrary",), 56),
        name="moe_experts",
    )(nt, te, tok, tok, dst, xn, wgu, bgu, wd, bd)


def _route(idx, t_all, tm):
    a_all = TOP_K * t_all
    bits = int(np.ceil(np.log2(a_all)))
    assert N_EXPERTS << bits < 2 ** 31
    ntmax = (a_all + N_EXPERTS * (tm - 1)) // tm
    e_flat = idx.T.reshape(-1)
    key = jnp.sort((e_flat << bits) + jnp.arange(a_all, dtype=jnp.int32))
    a_sorted = key & ((1 << bits) - 1)
    off = jnp.searchsorted(key, jnp.arange(N_EXPERTS + 1, dtype=jnp.int32) << bits).astype(jnp.int32)
    cnt = off[1:] - off[:-1]
    tiles_e = (cnt + tm - 1) // tm
    tile_end = jnp.cumsum(tiles_e)
    tile_start = tile_end - tiles_e
    nt = tile_end[-1:]
    ti = jnp.arange(ntmax, dtype=jnp.int32)
    te = jnp.minimum(jnp.searchsorted(tile_end, ti, side="right"), N_EXPERTS - 1).astype(jnp.int32)
    r = jnp.arange(tm, dtype=jnp.int32)[None, :]
    within = (ti - tile_start[te])[:, None] * tm + r
    valid = (within < cnt[te][:, None]) & (ti < nt)[:, None]
    a = a_sorted[jnp.clip(off[te][:, None] + within, 0, a_all - 1)]
    tok = jnp.where(valid, a % t_all, 0)
    spare = a_all + (ti % 2)[:, None] * tm + r
    dst = jnp.where(valid, a, spare)
    return nt.astype(jnp.int32), te, tok.reshape(ntmax, 1, tm), dst.reshape(ntmax, 1, tm)


def _combine_kernel(h_ref, y0_ref, y1_ref, y2_ref, y3_ref, gate_ref, g_ref, o_ref):
    acc = h_ref[...]
    gates = gate_ref[...]
    for k, y_ref in enumerate((y0_ref, y1_ref, y2_ref, y3_ref)):
        acc = acc + gates[:, k:k + 1] * y_ref[...]
    ms = jnp.mean(acc * acc, axis=-1, keepdims=True)
    o_ref[...] = acc * lax.rsqrt(ms + EPS) * g_ref[...]


def _combine(h, y, gates, g, t_all, t0, *, tm):
    t, d = h.shape
    yspec = lambda k: pl.BlockSpec((tm, d), lambda i: ((k * t_all + t0) // tm + i, 0))
    return pl.pallas_call(
        _combine_kernel,
        out_shape=jax.ShapeDtypeStruct((t, d), F32),
        grid=(t // tm,),
        in_specs=[pl.BlockSpec((tm, d), lambda i: (i, 0)),
                  yspec(0), yspec(1), yspec(2), yspec(3),
                  pl.BlockSpec((tm, LANES), lambda i: (t0 // tm + i, 0)),
                  pl.BlockSpec((1, d), lambda i: (0, 0))],
        out_specs=pl.BlockSpec((tm, d), lambda i: (i, 0)),
        compiler_params=_params(("parallel",), 56),
        name="combine",
    )(h, y, y, y, y, gates, g.reshape(1, d))


def _mixer_and_router(x, mem, p):
    b, s, d = x.shape
    t = b * s
    x2 = x.reshape(t, d)
    proj = _norm_proj(x2, p["g_mix"], p["w_in"], _rope_tables(s), s)
    proj3 = proj.reshape(b, s, IN_WIDTH)
    kv = _norm_proj(mem.reshape(b * N_MEM, d), p["g_mem"], p["w_mem_kv"], tm_pref=256)
    gg = p["g_group"]
    oa = _window_attn(proj3, p["attn_sink"], gg[:ATTN_WIDTH])
    osg = _sgu(proj3, p["w_spatial"], p["b_spatial"], p["g_sgu"], gg[ATTN_WIDTH:ATTN_WIDTH + SGU_WIDTH])
    ox = _mem_xattn(proj3, kv.reshape(b, N_MEM, 2 * XATTN_WIDTH), gg[ATTN_WIDTH + SGU_WIDTH:])
    h = _out_proj(oa.reshape(t, -1), osg.reshape(t, -1), ox.reshape(t, -1), p["w_out"], x2)
    xn, idx, gates = _router(h, p["g_ffn"], p["w_router"], p["b_router"])
    return h, xn, idx, gates


def kernel(x_prompt, x_sample, mem_prompt, mem_sample, g_mix, w_in, attn_sink, g_sgu, w_spatial, b_spatial,
           g_mem, w_mem_kv, g_group, w_out, g_ffn, w_router, b_router, w_gate_up, b_gate_up, w_down, b_down,
           g_final):
    assert g_mix.shape[0] == 1
    w_in0 = w_in[0]
    q, k, v, u, vs, xq = jnp.split(w_in0, [2048, 2304, 2560, 3584, 4608], axis=1)
    p = {
        "g_mix": g_mix[0],
        "w_in": jnp.concatenate([q, u, vs, xq, k, v], axis=1).astype(BF16),
        "attn_sink": attn_sink[0],
        "g_sgu": g_sgu[0],
        "w_spatial": w_spatial[0].astype(BF16),
        "b_spatial": jnp.repeat(b_spatial[0].T, SGU_GROUP_DIM, axis=1),
        "g_mem": g_mem[0],
        "w_mem_kv": w_mem_kv[0].astype(BF16),
        "g_group": g_group[0],
        "w_out": w_out[0].astype(BF16),
        "g_ffn": g_ffn[0],
        "w_router": w_router[0].astype(BF16),
        "b_router": b_router[0],
    }
    groups = [(x_prompt, mem_prompt), (x_sample, mem_sample)]
    parts = [_mixer_and_router(x, mem, p) for x, mem in groups]
    t_sizes = [h.shape[0] for h, _, _, _ in parts]
    t_all = sum(t_sizes)
    xn_all = jnp.concatenate([pt[1] for pt in parts], axis=0)
    idx_all = jnp.concatenate([pt[2][:, :TOP_K] for pt in parts], axis=0)
    gates_all = jnp.concatenate([pt[3] for pt in parts], axis=0)

    tm = 256
    assert all(ts % tm == 0 for ts in t_sizes)
    nt, te, tok, dst = _route(idx_all, t_all, tm)
    y = _moe(xn_all, nt, te, tok, dst,
             w_gate_up[0].astype(BF16), b_gate_up[0][:, None, :], w_down[0].astype(BF16), b_down[0][:, None, :],
             tm=tm, y_rows=TOP_K * t_all + 2 * tm)
    outs = []
    t0 = 0
    for (x, _), (h, _, _, _), ts in zip(groups, parts, t_sizes):
        outs.append(_combine(h, y, gates_all, g_final, t_all, t0, tm=tm).reshape(x.shape))
        t0 += ts
    return tuple(outs)
```

```python
import functools

import jax
import jax.numpy as jnp
import numpy as np
from jax import lax
from jax.experimental import pallas as pl
from jax.experimental.pallas import tpu as pltpu

D_MODEL = 4096
N_MEM = 256
HEAD_DIM = 64
ATTN_WIDTH = 2048
ATTN_HEADS = 32
KV_HEADS = 4
KV_WIDTH = 256
WINDOW = 128
BLOCK = 128
ROPE_DIM = 16
ROPE_THETA = 500000.0
SGU_WIDTH = 1024
SGU_GROUPS = 4
SGU_GROUP_DIM = 256
CHUNK = 128
XATTN_WIDTH = 1024
XATTN_HEADS = 4
XATTN_HEAD_DIM = 256
IN_WIDTH = 5632
N_EXPERTS = 32
TOP_K = 4
D_EXPERT = 512
SWIGLU_LIMIT = 7.0
SWIGLU_ALPHA = 1.702
EPS = 1e-5
NEG_INF = -1e30

LANES = 128
COL_Q, COL_U, COL_VS, COL_XQ, COL_K, COL_V = 0, 2048, 3072, 4096, 5120, 5376

F32 = jnp.float32
BF16 = jnp.bfloat16
MIB = 1024 * 1024


def _params(semantics, vmem_mib):
    return pltpu.CompilerParams(dimension_semantics=semantics, vmem_limit_bytes=vmem_mib * MIB)


def _tile(n, pref):
    t = min(n, pref)
    while n % t or t % LANES:
        t -= LANES
    assert t > 0
    return t


def _rope_group(a, c, s1, s2):
    return a * c + pltpu.roll(a, LANES - ROPE_DIM // 2, 1) * s1 + pltpu.roll(a, ROPE_DIM // 2, 1) * s2


def _norm_proj_kernel(*refs, rope, tn):
    if rope:
        x_ref, g_ref, w_ref, c_ref, s1_ref, s2_ref, o_ref, xn_ref = refs
    else:
        x_ref, g_ref, w_ref, o_ref, xn_ref = refs
    j = pl.program_id(1)

    @pl.when(j == 0)
    def _():
        x = x_ref[...]
        ms = jnp.mean(x * x, axis=-1, keepdims=True)
        xn_ref[...] = (x * lax.rsqrt(ms + EPS) * g_ref[...]).astype(BF16)

    acc = jnp.dot(xn_ref[...], w_ref[...], preferred_element_type=F32)
    if not rope:
        o_ref[...] = acc.astype(o_ref.dtype)
        return

    ngroups = tn // LANES
    q_tiles = ATTN_WIDTH // tn
    k_tile = COL_K // tn
    k_groups = KV_WIDTH // LANES

    def store(n_rope):
        c, s1, s2 = c_ref[...], s1_ref[...], s2_ref[...]
        for gidx in range(ngroups):
            a = acc[:, gidx * LANES:(gidx + 1) * LANES]
            if gidx < n_rope:
                a = _rope_group(a, c, s1, s2)
            o_ref[:, gidx * LANES:(gidx + 1) * LANES] = a.astype(o_ref.dtype)

    @pl.when(j < q_tiles)
    def _():
        store(ngroups)

    @pl.when(j == k_tile)
    def _():
        store(k_groups)

    @pl.when(jnp.logical_and(j >= q_tiles, j != k_tile))
    def _():
        store(0)


def _norm_proj(x, g, w, rope_tabs=None, seq=None, *, tm_pref=512, tn=512):
    t, d = x.shape
    n = w.shape[1]
    tm = _tile(t if seq is None else seq, tm_pref)
    rope = rope_tabs is not None
    in_specs = [
        pl.BlockSpec((tm, d), lambda i, j: (i, 0)),
        pl.BlockSpec((1, d), lambda i, j: (0, 0)),
        pl.BlockSpec((d, tn), lambda i, j: (0, j)),
    ]
    args = [x, g.reshape(1, d), w]
    if rope:
        assert COL_K % tn == 0 and ATTN_WIDTH % tn == 0 and tn >= KV_WIDTH
        sblocks = seq // tm
        tab_spec = pl.BlockSpec((tm, LANES), lambda i, j: (i % sblocks, 0))
        in_specs += [tab_spec, tab_spec, tab_spec]
        args += list(rope_tabs)
    return pl.pallas_call(
        functools.partial(_norm_proj_kernel, rope=rope, tn=tn),
        out_shape=jax.ShapeDtypeStruct((t, n), BF16),
        grid=(t // tm, n // tn),
        in_specs=in_specs,
        out_specs=pl.BlockSpec((tm, tn), lambda i, j: (i, j)),
        scratch_shapes=[pltpu.VMEM((tm, d), BF16)],
        compiler_params=_params(("parallel", "arbitrary"), 48),
        name="norm_proj_rope" if rope else "norm_proj",
    )(*args)


def _rope_tables(seq):
    half = ROPE_DIM // 2
    inv_freq = ROPE_THETA ** (-jnp.arange(0, ROPE_DIM, 2, dtype=F32) / ROPE_DIM)
    ang = jnp.arange(seq, dtype=F32)[:, None] * inv_freq[None, :]
    cos, sin = jnp.cos(ang), jnp.sin(ang)
    dim = np.arange(LANES) % HEAD_DIM
    sel = dim % half
    cos_l, sin_l = cos[:, sel], sin[:, sel]
    c = jnp.where(dim < ROPE_DIM, cos_l, 1.0)
    s1 = jnp.where(dim < half, -sin_l, 0.0)
    s2 = jnp.where((dim >= half) & (dim < ROPE_DIM), sin_l, 0.0)
    return c.astype(F32), s1.astype(F32), s2.astype(F32)


def _window_attn_kernel(sink_ref, q_ref, kp_ref, kc_ref, kn_ref, vp_ref, vc_ref, vn_ref, gg_ref,
                        o_ref, obuf_ref, *, nb):
    n = pl.program_id(1)
    band = 3 * BLOCK
    qi = lax.broadcasted_iota(jnp.int32, (BLOCK, band), 0)
    ki = lax.broadcasted_iota(jnp.int32, (BLOCK, band), 1)
    rel = ki - BLOCK - qi
    lo = jnp.where(n == 0, BLOCK, 0)
    hi = jnp.where(n == nb - 1, 2 * BLOCK, band)
    valid = (jnp.abs(rel) <= WINDOW) & (ki >= lo) & (ki < hi)
    lane_k = lax.broadcasted_iota(jnp.int32, (band, LANES), 1)
    left_k = lane_k < HEAD_DIM
    lane_q = lax.broadcasted_iota(jnp.int32, (BLOCK, LANES), 1)
    left_q = lane_q < HEAD_DIM
    ones_l = jnp.where(left_k, 1.0, 0.0).astype(F32)
    ones_r = 1.0 - ones_l
    scale = HEAD_DIM ** -0.5
    ssq = jnp.zeros((BLOCK, 1), F32)

    for h in range(KV_HEADS):
        slab = (h // 2) * LANES
        k3 = jnp.concatenate([r[:, slab:slab + LANES] for r in (kp_ref, kc_ref, kn_ref)], axis=0).astype(F32)
        v3 = jnp.concatenate([r[:, slab:slab + LANES] for r in (vp_ref, vc_ref, vn_ref)], axis=0).astype(F32)
        k3r = pltpu.roll(k3, HEAD_DIM, 1)
        v3r = pltpu.roll(v3, HEAD_DIM, 1)
        if h % 2 == 0:
            ka, kb, va, vb = k3, k3r, v3, v3r
        else:
            ka, kb, va, vb = k3r, k3, v3r, v3
        kbd = jnp.concatenate([jnp.where(left_k, ka, 0.0), jnp.where(left_k, 0.0, kb)], axis=0).astype(BF16)
        vbd = jnp.concatenate(
            [jnp.concatenate([jnp.where(left_k, va, 0.0), ones_l], axis=1),
             jnp.concatenate([jnp.where(left_k, 0.0, vb), ones_r], axis=1)], axis=0).astype(BF16)
        for p in range(ATTN_HEADS // KV_HEADS // 2):
            g = h * 4 + p
            q2 = q_ref[:, g * LANES:(g + 1) * LANES]
            s2 = lax.dot_general(q2, kbd, (((1,), (1,)), ((), ())), preferred_element_type=F32)
            sa = jnp.where(valid, s2[:, :band] * scale, NEG_INF)
            sb = jnp.where(valid, s2[:, band:] * scale, NEG_INF)
            sink_a = sink_ref[2 * g]
            sink_b = sink_ref[2 * g + 1]
            ma = jnp.maximum(jnp.max(sa, axis=-1, keepdims=True), sink_a)
            mb = jnp.maximum(jnp.max(sb, axis=-1, keepdims=True), sink_b)
            p2 = jnp.concatenate([jnp.exp(sa - ma), jnp.exp(sb - mb)], axis=1).astype(BF16)
            o2 = jnp.dot(p2, vbd, preferred_element_type=F32)
            den = o2[:, LANES:] + jnp.where(left_q, jnp.exp(sink_a - ma), jnp.exp(sink_b - mb))
            out = o2[:, :LANES] / den
            ssq = ssq + jnp.sum(out * out, axis=-1, keepdims=True)
            obuf_ref[:, g * LANES:(g + 1) * LANES] = out

    r = lax.rsqrt(ssq * (1.0 / ATTN_WIDTH) + EPS)
    o_ref[...] = (obuf_ref[...] * r * gg_ref[...]).astype(o_ref.dtype)


def _window_attn(proj3, sink, gg):
    b, s, _ = proj3.shape
    nb = s // BLOCK
    kcol, vcol = COL_K // KV_WIDTH, COL_V // KV_WIDTH

    def band_spec(col, shift):
        return pl.BlockSpec((None, BLOCK, KV_WIDTH),
                            lambda bi, n: (bi, jnp.clip(n + shift, 0, nb - 1), col))

    return pl.pallas_call(
        functools.partial(_window_attn_kernel, nb=nb),
        out_shape=jax.ShapeDtypeStruct((b, s, ATTN_WIDTH), BF16),
        grid=(b, nb),
        in_specs=[
            pl.BlockSpec(memory_space=pltpu.SMEM),
            pl.BlockSpec((None, BLOCK, ATTN_WIDTH), lambda bi, n: (bi, n, 0)),
            band_spec(kcol, -1), band_spec(kcol, 0), band_spec(kcol, 1),
            band_spec(vcol, -1), band_spec(vcol, 0), band_spec(vcol, 1),
            pl.BlockSpec((1, ATTN_WIDTH), lambda bi, n: (0, 0)),
        ],
        out_specs=pl.BlockSpec((None, BLOCK, ATTN_WIDTH), lambda bi, n: (bi, n, 0)),
        scratch_shapes=[pltpu.VMEM((BLOCK, ATTN_WIDTH), F32)],
        compiler_params=_params(("parallel", "parallel"), 32),
        name="window_attn",
    )(sink, proj3, proj3, proj3, proj3, proj3, proj3, proj3, gg.reshape(1, ATTN_WIDTH))


def _sgu_kernel(u_ref, v_ref, ws_ref, bs_ref, gs_ref, gg_ref, o_ref, *, chunks):
    for c in range(chunks):
        rows = slice(c * CHUNK, (c + 1) * CHUNK)
        ug = jax.nn.gelu(u_ref[rows, :].astype(F32))
        vg = jax.nn.gelu(v_ref[rows, :].astype(F32))
        mixed = []
        for h in range(SGU_GROUPS):
            cols = slice(h * SGU_GROUP_DIM, (h + 1) * SGU_GROUP_DIM)
            vh = vg[:, cols]
            r = lax.rsqrt(jnp.mean(vh * vh, axis=-1, keepdims=True) + EPS)
            vn = (vh * r * gs_ref[:, cols]).astype(BF16)
            mixed.append(jnp.dot(ws_ref[h], vn, preferred_element_type=F32))
        o = ug * (jnp.concatenate(mixed, axis=1) + bs_ref[...])
        r = lax.rsqrt(jnp.mean(o * o, axis=-1, keepdims=True) + EPS)
        o_ref[rows, :] = (o * r * gg_ref[...]).astype(o_ref.dtype)


def _sgu(proj3, ws, bs_full, gs, gg, *, rows_pref=512):
    b, s, _ = proj3.shape
    tr = _tile(s, rows_pref)
    const = lambda bi, n: (0, 0)
    return pl.pallas_call(
        functools.partial(_sgu_kernel, chunks=tr // CHUNK),
        out_shape=jax.ShapeDtypeStruct((b, s, SGU_WIDTH), BF16),
        grid=(b, s // tr),
        in_specs=[
            pl.BlockSpec((None, tr, SGU_WIDTH), lambda bi, n: (bi, n, COL_U // SGU_WIDTH)),
            pl.BlockSpec((None, tr, SGU_WIDTH), lambda bi, n: (bi, n, COL_VS // SGU_WIDTH)),
            pl.BlockSpec((SGU_GROUPS, CHUNK, CHUNK), lambda bi, n: (0, 0, 0)),
            pl.BlockSpec((CHUNK, SGU_WIDTH), const),
            pl.BlockSpec((1, SGU_WIDTH), const),
            pl.BlockSpec((1, SGU_WIDTH), const),
        ],
        out_specs=pl.BlockSpec((None, tr, SGU_WIDTH), lambda bi, n: (bi, n, 0)),
        compiler_params=_params(("parallel", "parallel"), 32),
        name="sgu",
    )(proj3, proj3, ws, bs_full, gs.reshape(1, SGU_WIDTH), gg.reshape(1, SGU_WIDTH))


def _mem_xattn_kernel(q_ref, kv_ref, gg_ref, o_ref):
    scale = XATTN_HEAD_DIM ** -0.5
    outs = []
    ssq = None
    for h in range(XATTN_HEADS):
        cols = slice(h * XATTN_HEAD_DIM, (h + 1) * XATTN_HEAD_DIM)
        vcols = slice(XATTN_WIDTH + h * XATTN_HEAD_DIM, XATTN_WIDTH + (h + 1) * XATTN_HEAD_DIM)
        s = lax.dot_general(q_ref[:, cols], kv_ref[:, cols], (((1,), (1,)), ((), ())),
                            preferred_element_type=F32) * scale
        m = jnp.max(s, axis=-1, keepdims=True)
        p = jnp.exp(s - m)
        l = jnp.sum(p, axis=-1, keepdims=True)
        o = jnp.dot(p.astype(BF16), kv_ref[:, vcols], preferred_element_type=F32) / l
        sq = jnp.sum(o * o, axis=-1, keepdims=True)
        ssq = sq if ssq is None else ssq + sq
        outs.append(o)
    r = lax.rsqrt(ssq * (1.0 / XATTN_WIDTH) + EPS)
    for h in range(XATTN_HEADS):
        cols = slice(h * XATTN_HEAD_DIM, (h + 1) * XATTN_HEAD_DIM)
        o_ref[:, cols] = (outs[h] * r * gg_ref[:, cols]).astype(o_ref.dtype)


def _mem_xattn(proj3, kv3, gg, *, rows_pref=512):
    b, s, _ = proj3.shape
    tq = _tile(s, rows_pref)
    return pl.pallas_call(
        _mem_xattn_kernel,
        out_shape=jax.ShapeDtypeStruct((b, s, XATTN_WIDTH), BF16),
        grid=(b, s // tq),
        in_specs=[
            pl.BlockSpec((None, tq, XATTN_WIDTH), lambda bi, n: (bi, n, COL_XQ // XATTN_WIDTH)),
            pl.BlockSpec((None, N_MEM, 2 * XATTN_WIDTH), lambda bi, n: (bi, 0, 0)),
            pl.BlockSpec((1, XATTN_WIDTH), lambda bi, n: (0, 0)),
        ],
        out_specs=pl.BlockSpec((None, tq, XATTN_WIDTH), lambda bi, n: (bi, n, 0)),
        compiler_params=_params(("parallel", "parallel"), 32),
        name="mem_xattn",
    )(proj3, kv3, gg.reshape(1, XATTN_WIDTH))


def _out_proj_kernel(oa_ref, os_ref, ox_ref, w_ref, x_ref, h_ref):
    a1, a2 = ATTN_WIDTH, ATTN_WIDTH + SGU_WIDTH
    acc = jnp.dot(oa_ref[...], w_ref[:a1, :], preferred_element_type=F32)
    acc += jnp.dot(os_ref[...], w_ref[a1:a2, :], preferred_element_type=F32)
    acc += jnp.dot(ox_ref[...], w_ref[a2:, :], preferred_element_type=F32)
    h_ref[...] = x_ref[...] + acc


def _out_proj(oa, osg, ox, w, x, *, tm_pref=512, tn=1024):
    t, d = x.shape
    tm = _tile(t, tm_pref)
    return pl.pallas_call(
        _out_proj_kernel,
        out_shape=jax.ShapeDtypeStruct((t, d), F32),
        grid=(t // tm, d // tn),
        in_specs=[
            pl.BlockSpec((tm, ATTN_WIDTH), lambda i, j: (i, 0)),
            pl.BlockSpec((tm, SGU_WIDTH), lambda i, j: (i, 0)),
            pl.BlockSpec((tm, XATTN_WIDTH), lambda i, j: (i, 0)),
            pl.BlockSpec((d, tn), lambda i, j: (0, j)),
            pl.BlockSpec((tm, tn), lambda i, j: (i, j)),
        ],
        out_specs=pl.BlockSpec((tm, tn), lambda i, j: (i, j)),
        compiler_params=_params(("parallel", "arbitrary"), 48),
        name="out_proj",
    )(oa, osg, ox, w, x)


SLAB_ROWS = D_MODEL // LANES
SLAB_PITCH = SLAB_ROWS + 4


def _router_kernel(h1_ref, h2_ref, g_ref, wr_ref, br_ref, xn_ref, meta_ref, gate_ref, cnt_ref, run_ref,
                   *, nb1, tm):
    i = pl.program_id(0)

    @pl.when(i == 0)
    def _():
        run_ref[...] = jnp.zeros_like(run_ref)

    def body(h_ref):
        x = h_ref[...]
        ms = jnp.mean(x * x, axis=-1, keepdims=True)
        xn = x * lax.rsqrt(ms + EPS) * g_ref[...]
        for c in range(SLAB_ROWS):
            xn_ref[pl.ds(c, tm, stride=SLAB_PITCH), :] = xn[:, c * LANES:(c + 1) * LANES]
        for c in range(SLAB_ROWS, SLAB_PITCH):
            xn_ref[pl.ds(c, tm, stride=SLAB_PITCH), :] = jnp.zeros((tm, LANES), F32)
        logits = jnp.dot(xn.astype(BF16), wr_ref[...], preferred_element_type=F32) + br_ref[...]
        lane_e = lax.broadcasted_iota(jnp.int32, (tm, N_EXPERTS), 1)
        lane_o = lax.broadcasted_iota(jnp.int32, (tm, LANES), 1)
        row = lax.broadcasted_iota(jnp.int32, (tm, tm), 0)
        col = lax.broadcasted_iota(jnp.int32, (tm, tm), 1)
        earlier = jnp.where(col < row, 1.0, 0.0).astype(BF16)
        vals, idxs = [], []
        l = logits
        for _ in range(TOP_K):
            m = jnp.max(l, axis=-1, keepdims=True)
            e = jnp.min(jnp.where(l == m, lane_e, N_EXPERTS), axis=-1, keepdims=True)
            vals.append(m)
            idxs.append(e)
            l = jnp.where(lane_e == e, -jnp.inf, l)
        es = [jnp.exp(v - vals[0]) for v in vals]
        den = es[0] + es[1] + es[2] + es[3]
        base = run_ref[...]
        meta = jnp.zeros((tm, LANES), jnp.int32)
        gate_out = jnp.zeros((tm, LANES), F32)
        for k in range(TOP_K):
            onehot = jnp.where(lane_e == idxs[k], 1.0, 0.0)
            before = jnp.dot(earlier, onehot.astype(BF16), preferred_element_type=F32)
            rank = jnp.sum(onehot * (before + base), axis=-1, keepdims=True)
            base = base + jnp.sum(onehot, axis=0, keepdims=True)
            meta = jnp.where(lane_o == k, idxs[k], meta)
            meta = jnp.where(lane_o == TOP_K + k, rank.astype(jnp.int32), meta)
            gate_out = jnp.where(lane_o == k, es[k] / den, gate_out)
        run_ref[...] = base
        cnt_ref[...] = base
        meta_ref[...] = meta
        gate_ref[...] = gate_out

    @pl.when(i < nb1)
    def _():
        body(h1_ref)

    @pl.when(i >= nb1)
    def _():
        body(h2_ref)


def _router(h1, h2, g, wr, br, *, tm):
    t1, d = h1.shape
    t2 = h2.shape[0]
    nb1, nb2 = t1 // tm, t2 // tm
    t_all = t1 + t2
    const = lambda i: (0, 0)
    return pl.pallas_call(
        functools.partial(_router_kernel, nb1=nb1, tm=tm),
        out_shape=(jax.ShapeDtypeStruct((t_all * SLAB_PITCH, LANES), F32),
                   jax.ShapeDtypeStruct((t_all, LANES), jnp.int32),
                   jax.ShapeDtypeStruct((t_all, LANES), F32),
                   jax.ShapeDtypeStruct((1, N_EXPERTS), F32)),
        grid=(nb1 + nb2,),
        in_specs=[
            pl.BlockSpec((tm, d), lambda i: (jnp.minimum(i, nb1 - 1), 0)),
            pl.BlockSpec((tm, d), lambda i: (jnp.maximum(i - nb1, 0), 0)),
            pl.BlockSpec((1, d), const),
            pl.BlockSpec((d, N_EXPERTS), const),
            pl.BlockSpec((1, N_EXPERTS), const),
        ],
        out_specs=(pl.BlockSpec((tm * SLAB_PITCH, LANES), lambda i: (i, 0)),
                   pl.BlockSpec((tm, LANES), lambda i: (i, 0)),
                   pl.BlockSpec((tm, LANES), lambda i: (i, 0)),
                   pl.BlockSpec((1, N_EXPERTS), const)),
        scratch_shapes=[pltpu.VMEM((1, N_EXPERTS), F32)],
        compiler_params=_params(("arbitrary",), 48),
        name="router",
    )(h1, h2, g.reshape(1, d), wr, br.reshape(1, N_EXPERTS))


def _moe_kernel(nt_ref, te_ref, tok_ref, tokn_ref, dstp_ref, xn_hbm, wgu_ref, bgu_ref, wd_ref, bd_ref,
                y_hbm, xs0, xs1, yb0, yb1, xb, gsem, ssem, *, tm, spare0):
    del te_ref
    i = pl.program_id(0)
    nt = nt_ref[0]
    rows = range(tm)

    def gather(idx_ref, r, xs, s):
        return pltpu.make_async_copy(xn_hbm.at[pl.ds(idx_ref[0, 0, r] * SLAB_PITCH, SLAB_ROWS), :],
                                     xs.at[pl.ds(r * SLAB_PITCH, SLAB_ROWS), :], gsem.at[s])

    def scatter(r, yb, s):
        return pltpu.make_async_copy(yb.at[pl.ds(r, 1), :], y_hbm.at[pl.ds(dstp_ref[0, 0, r], 1), :], ssem.at[s])

    def step(s, xs, xs_next, yb, yb_prev):
        for r in rows:
            gather(tok_ref, r, xs, s).wait()

        @pl.when(i >= 1)
        def _():
            for r in rows:
                scatter(r, yb, s).wait()

        for r in rows:
            gather(tokn_ref, r, xs_next, 1 - s).start()
        for r in rows:
            scatter(r, yb_prev, 1 - s).start()
        for c in range(SLAB_ROWS):
            xb[:, c * LANES:(c + 1) * LANES] = xs[pl.ds(c, tm, stride=SLAB_PITCH), :].astype(BF16)
        gu = jnp.dot(xb[...], wgu_ref[...], preferred_element_type=F32) + bgu_ref[...]
        gate = jnp.minimum(gu[:, :D_EXPERT], SWIGLU_LIMIT)
        up = jnp.clip(gu[:, D_EXPERT:], -SWIGLU_LIMIT, SWIGLU_LIMIT)
        hid = (up + 1.0) * (gate * jax.nn.sigmoid(SWIGLU_ALPHA * gate))
        yb[...] = jnp.dot(hid.astype(BF16), wd_ref[...], preferred_element_type=F32) + bd_ref[...]

    @pl.when(i == 0)
    def _():
        yb1[...] = jnp.zeros_like(yb1)
        for r in rows:
            gather(tok_ref, r, xs0, 0).start()

    @pl.when(jnp.logical_and(i < nt, i % 2 == 0))
    def _():
        step(0, xs0, xs1, yb0, yb1)

    @pl.when(jnp.logical_and(i < nt, i % 2 == 1))
    def _():
        step(1, xs1, xs0, yb1, yb0)

    def drain(s, xs, yb, yb_prev):
        for r in rows:
            gather(tok_ref, r, xs, s).wait()
        for r in rows:
            scatter(r, yb, s).wait()
        for r in rows:
            scatter(r, yb_prev, 1 - s).start()
        for r in rows:
            scatter(r, yb_prev, 1 - s).wait()
        for q, ybq in enumerate((yb0, yb1)):
            fill = pltpu.make_async_copy(ybq, y_hbm.at[pl.ds(spare0 + q * tm, tm), :], ssem.at[q])
            fill.start()
            fill.wait()

    @pl.when(jnp.logical_and(i == nt, i % 2 == 0))
    def _():
        drain(0, xs0, yb0, yb1)

    @pl.when(jnp.logical_and(i == nt, i % 2 == 1))
    def _():
        drain(1, xs1, yb1, yb0)


def _moe(xn, nt, te, tok, dstp, wgu, bgu, wd, bd, *, tm, y_rows, spare0):
    ntmax = te.shape[0]
    d = D_MODEL
    row_spec = lambda shift, nrows: pl.BlockSpec(
        (1, 1, tm), lambda i, nt_, te_: (jnp.minimum(i + shift, nrows - 1), 0, 0), memory_space=pltpu.SMEM)
    expert = lambda i, nt_, te_: (te_[jnp.minimum(i, ntmax - 1)], 0, 0)
    grid_spec = pltpu.PrefetchScalarGridSpec(
        num_scalar_prefetch=2,
        grid=(ntmax + 1,),
        in_specs=[
            row_spec(0, ntmax), row_spec(1, ntmax), row_spec(0, ntmax + 1),
            pl.BlockSpec(memory_space=pl.ANY),
            pl.BlockSpec((None, d, 2 * D_EXPERT), expert),
            pl.BlockSpec((None, 1, 2 * D_EXPERT), expert),
            pl.BlockSpec((None, D_EXPERT, d), expert),
            pl.BlockSpec((None, 1, d), expert),
        ],
        out_specs=pl.BlockSpec(memory_space=pl.ANY),
        scratch_shapes=[
            pltpu.VMEM((tm * SLAB_PITCH, LANES), F32),
            pltpu.VMEM((tm * SLAB_PITCH, LANES), F32),
            pltpu.VMEM((tm, d), F32),
            pltpu.VMEM((tm, d), F32),
            pltpu.VMEM((tm, d), BF16),
            pltpu.SemaphoreType.DMA((2,)),
            pltpu.SemaphoreType.DMA((2,)),
        ],
    )
    return pl.pallas_call(
        functools.partial(_moe_kernel, tm=tm, spare0=spare0),
        out_shape=jax.ShapeDtypeStruct((y_rows, d), F32),
        grid_spec=grid_spec,
        compiler_params=_params(("arbitrary",), 58),
        name="moe_experts",
    )(nt, te, tok, tok, dstp, xn, wgu, bgu, wd, bd)


TOK_BITS = 20


def _route(meta, cnt, t_all, tm):
    a_all = TOP_K * t_all
    assert t_all < (1 << TOK_BITS)
    ntmax = (a_all + N_EXPERTS * (tm - 1)) // tm
    cnt = cnt.reshape(N_EXPERTS).astype(jnp.int32)
    tiles_e = (cnt + tm - 1) // tm
    tile_end = jnp.cumsum(tiles_e)
    tile_start = tile_end - tiles_e
    nt = tile_end[-1:]
    ti = jnp.arange(ntmax, dtype=jnp.int32)
    te = jnp.minimum(jnp.sum((tile_end[None, :] <= ti[:, None]).astype(jnp.int32), axis=1), N_EXPERTS - 1)
    idx, rank = meta[:, :TOP_K], meta[:, TOP_K:2 * TOP_K]
    slot = tile_start[idx] * tm + rank
    val = jnp.arange(t_all, dtype=jnp.int32)[:, None] + (jnp.arange(TOP_K, dtype=jnp.int32) << TOK_BITS)[None, :]
    table = jnp.full((ntmax * tm,), -1, jnp.int32).at[slot.reshape(-1)].set(
        val.reshape(-1), unique_indices=True, mode="promise_in_bounds").reshape(ntmax, tm)
    valid = table >= 0
    tok = jnp.where(valid, table & ((1 << TOK_BITS) - 1), 0)
    r = jnp.arange(tm, dtype=jnp.int32)[None, :]
    spare = a_all + (ti % 2)[:, None] * tm + r
    dst = jnp.where(valid, (table >> TOK_BITS) * t_all + tok, spare)
    dstp = jnp.concatenate([a_all + tm + r, dst], axis=0)
    return nt.astype(jnp.int32), te, tok.reshape(ntmax, 1, tm), dstp.reshape(ntmax + 1, 1, tm)


def _combine_kernel(h_ref, y0_ref, y1_ref, y2_ref, y3_ref, gate_ref, g_ref, o_ref):
    acc = h_ref[...]
    gates = gate_ref[...]
    for k, y_ref in enumerate((y0_ref, y1_ref, y2_ref, y3_ref)):
        acc = acc + gates[:, k:k + 1] * y_ref[...]
    ms = jnp.mean(acc * acc, axis=-1, keepdims=True)
    o_ref[...] = acc * lax.rsqrt(ms + EPS) * g_ref[...]


def _combine(h, y, gates, g, t_all, t0, *, tm):
    t, d = h.shape
    yspec = lambda k: pl.BlockSpec((tm, d), lambda i: ((k * t_all + t0) // tm + i, 0))
    return pl.pallas_call(
        _combine_kernel,
        out_shape=jax.ShapeDtypeStruct((t, d), F32),
        grid=(t // tm,),
        in_specs=[pl.BlockSpec((tm, d), lambda i: (i, 0)),
                  yspec(0), yspec(1), yspec(2), yspec(3),
                  pl.BlockSpec((tm, LANES), lambda i: (t0 // tm + i, 0)),
                  pl.BlockSpec((1, d), lambda i: (0, 0))],
        out_specs=pl.BlockSpec((tm, d), lambda i: (i, 0)),
        compiler_params=_params(("parallel",), 56),
        name="combine",
    )(h, y, y, y, y, gates, g.reshape(1, d))


def _mixer(x, mem, p):
    b, s, d = x.shape
    t = b * s
    x2 = x.reshape(t, d)
    proj = _norm_proj(x2, p["g_mix"], p["w_in"], _rope_tables(s), s)
    proj3 = proj.reshape(b, s, IN_WIDTH)
    kv = _norm_proj(mem.reshape(b * N_MEM, d), p["g_mem"], p["w_mem_kv"], tm_pref=256)
    gg = p["g_group"]
    oa = _window_attn(proj3, p["attn_sink"], gg[:ATTN_WIDTH])
    osg = _sgu(proj3, p["w_spatial"], p["b_spatial"], p["g_sgu"], gg[ATTN_WIDTH:ATTN_WIDTH + SGU_WIDTH])
    ox = _mem_xattn(proj3, kv.reshape(b, N_MEM, 2 * XATTN_WIDTH), gg[ATTN_WIDTH + SGU_WIDTH:])
    return _out_proj(oa.reshape(t, -1), osg.reshape(t, -1), ox.reshape(t, -1), p["w_out"], x2)


def kernel(x_prompt, x_sample, mem_prompt, mem_sample, g_mix, w_in, attn_sink, g_sgu, w_spatial, b_spatial,
           g_mem, w_mem_kv, g_group, w_out, g_ffn, w_router, b_router, w_gate_up, b_gate_up, w_down, b_down,
           g_final):
    assert g_mix.shape[0] == 1
    q, k, v, u, vs, xq = jnp.split(w_in[0], [2048, 2304, 2560, 3584, 4608], axis=1)
    p = {
        "g_mix": g_mix[0],
        "w_in": jnp.concatenate([q, u, vs, xq, k, v], axis=1).astype(BF16),
        "attn_sink": attn_sink[0],
        "g_sgu": g_sgu[0],
        "w_spatial": w_spatial[0].astype(BF16),
        "b_spatial": jnp.repeat(b_spatial[0].T, SGU_GROUP_DIM, axis=1),
        "g_mem": g_mem[0],
        "w_mem_kv": w_mem_kv[0].astype(BF16),
        "g_group": g_group[0],
        "w_out": w_out[0].astype(BF16),
    }
    h1 = _mixer(x_prompt, mem_prompt, p)
    h2 = _mixer(x_sample, mem_sample, p)
    t1, t2 = h1.shape[0], h2.shape[0]
    t_all = t1 + t2
    tm = 256
    assert t1 % tm == 0 and t2 % tm == 0
    xn, meta, gates, cnt = _router(h1, h2, g_ffn[0], w_router[0].astype(BF16), b_router[0], tm=tm)
    nt, te, tok, dstp = _route(meta, cnt, t_all, tm)
    spare0 = TOP_K * t_all
    y = _moe(xn, nt, te, tok, dstp,
             w_gate_up[0].astype(BF16), b_gate_up[0][:, None, :], w_down[0].astype(BF16), b_down[0][:, None, :],
             tm=tm, y_rows=spare0 + 2 * tm, spare0=spare0)
    out1 = _combine(h1, y, gates, g_final, t_all, 0, tm=tm).reshape(x_prompt.shape)
    out2 = _combine(h2, y, gates, g_final, t_all, t1, tm=tm).reshape(x_sample.shape)
    return out1, out2
```

```python
import functools

import jax
import jax.numpy as jnp
import numpy as np
from jax import lax
from jax.experimental import pallas as pl
from jax.experimental.pallas import tpu as pltpu

D_MODEL = 4096
N_MEM = 256
HEAD_DIM = 64
ATTN_WIDTH = 2048
ATTN_HEADS = 32
KV_HEADS = 4
KV_WIDTH = 256
WINDOW = 128
BLOCK = 128
ROPE_DIM = 16
ROPE_THETA = 500000.0
SGU_WIDTH = 1024
SGU_GROUPS = 4
SGU_GROUP_DIM = 256
CHUNK = 128
XATTN_WIDTH = 1024
XATTN_HEADS = 4
XATTN_HEAD_DIM = 256
IN_WIDTH = 5632
N_EXPERTS = 32
TOP_K = 4
D_EXPERT = 512
SWIGLU_LIMIT = 7.0
SWIGLU_ALPHA = 1.702
EPS = 1e-5
NEG_INF = -1e30

LANES = 128
COL_Q, COL_U, COL_VS, COL_XQ, COL_K, COL_V = 0, 2048, 3072, 4096, 5120, 5376

F32 = jnp.float32
BF16 = jnp.bfloat16
MIB = 1024 * 1024


def _params(semantics, vmem_mib):
    return pltpu.CompilerParams(dimension_semantics=semantics, vmem_limit_bytes=vmem_mib * MIB)


def _tile(n, pref):
    t = min(n, pref)
    while n % t or t % LANES:
        t -= LANES
    assert t > 0
    return t


def _rope_group(a, c, s1, s2):
    return a * c + pltpu.roll(a, LANES - ROPE_DIM // 2, 1) * s1 + pltpu.roll(a, ROPE_DIM // 2, 1) * s2


def _norm_proj_kernel(*refs, rope, tn):
    if rope:
        x_ref, g_ref, w_ref, c_ref, s1_ref, s2_ref, o_ref, xn_ref = refs
    else:
        x_ref, g_ref, w_ref, o_ref, xn_ref = refs
    j = pl.program_id(1)

    @pl.when(j == 0)
    def _():
        x = x_ref[...]
        ms = jnp.mean(x * x, axis=-1, keepdims=True)
        xn_ref[...] = (x * lax.rsqrt(ms + EPS) * g_ref[...]).astype(BF16)

    acc = jnp.dot(xn_ref[...], w_ref[...], preferred_element_type=F32)
    if not rope:
        o_ref[...] = acc.astype(o_ref.dtype)
        return

    ngroups = tn // LANES
    q_tiles = ATTN_WIDTH // tn
    k_tile = COL_K // tn
    k_groups = KV_WIDTH // LANES

    def store(n_rope):
        c, s1, s2 = c_ref[...], s1_ref[...], s2_ref[...]
        for gidx in range(ngroups):
            a = acc[:, gidx * LANES:(gidx + 1) * LANES]
            if gidx < n_rope:
                a = _rope_group(a, c, s1, s2)
            o_ref[:, gidx * LANES:(gidx + 1) * LANES] = a.astype(o_ref.dtype)

    @pl.when(j < q_tiles)
    def _():
        store(ngroups)

    @pl.when(j == k_tile)
    def _():
        store(k_groups)

    @pl.when(jnp.logical_and(j >= q_tiles, j != k_tile))
    def _():
        store(0)


def _norm_proj(x, g, w, rope_tabs=None, seq=None, *, tm_pref=512, tn=512):
    t, d = x.shape
    n = w.shape[1]
    tm = _tile(t if seq is None else seq, tm_pref)
    rope = rope_tabs is not None
    in_specs = [
        pl.BlockSpec((tm, d), lambda i, j: (i, 0)),
        pl.BlockSpec((1, d), lambda i, j: (0, 0)),
        pl.BlockSpec((d, tn), lambda i, j: (0, j)),
    ]
    args = [x, g.reshape(1, d), w]
    if rope:
        assert COL_K % tn == 0 and ATTN_WIDTH % tn == 0 and tn >= KV_WIDTH
        sblocks = seq // tm
        tab_spec = pl.BlockSpec((tm, LANES), lambda i, j: (i % sblocks, 0))
        in_specs += [tab_spec, tab_spec, tab_spec]
        args += list(rope_tabs)
    return pl.pallas_call(
        functools.partial(_norm_proj_kernel, rope=rope, tn=tn),
        out_shape=jax.ShapeDtypeStruct((t, n), BF16),
        grid=(t // tm, n // tn),
        in_specs=in_specs,
        out_specs=pl.BlockSpec((tm, tn), lambda i, j: (i, j)),
        scratch_shapes=[pltpu.VMEM((tm, d), BF16)],
        compiler_params=_params(("parallel", "arbitrary"), 48),
        name="norm_proj_rope" if rope else "norm_proj",
    )(*args)


def _rope_tables(seq):
    half = ROPE_DIM // 2
    inv_freq = ROPE_THETA ** (-jnp.arange(0, ROPE_DIM, 2, dtype=F32) / ROPE_DIM)
    ang = jnp.arange(seq, dtype=F32)[:, None] * inv_freq[None, :]
    cos, sin = jnp.cos(ang), jnp.sin(ang)
    dim = np.arange(LANES) % HEAD_DIM
    sel = dim % half
    cos_l, sin_l = cos[:, sel], sin[:, sel]
    c = jnp.where(dim < ROPE_DIM, cos_l, 1.0)
    s1 = jnp.where(dim < half, -sin_l, 0.0)
    s2 = jnp.where((dim >= half) & (dim < ROPE_DIM), sin_l, 0.0)
    return c.astype(F32), s1.astype(F32), s2.astype(F32)


def _window_attn_kernel(sink_ref, q_ref, kp_ref, kc_ref, kn_ref, vp_ref, vc_ref, vn_ref, gg_ref,
                        o_ref, obuf_ref, *, nb):
    n = pl.program_id(1)
    band = 3 * BLOCK
    qi = lax.broadcasted_iota(jnp.int32, (BLOCK, band), 0)
    ki = lax.broadcasted_iota(jnp.int32, (BLOCK, band), 1)
    rel = ki - BLOCK - qi
    lo = jnp.where(n == 0, BLOCK, 0)
    hi = jnp.where(n == nb - 1, 2 * BLOCK, band)
    valid = (jnp.abs(rel) <= WINDOW) & (ki >= lo) & (ki < hi)
    lane_k = lax.broadcasted_iota(jnp.int32, (band, LANES), 1)
    left_k = lane_k < HEAD_DIM
    lane_q = lax.broadcasted_iota(jnp.int32, (BLOCK, LANES), 1)
    left_q = lane_q < HEAD_DIM
    ones_l = jnp.where(left_k, 1.0, 0.0).astype(F32)
    ones_r = 1.0 - ones_l
    scale = HEAD_DIM ** -0.5
    ssq = jnp.zeros((BLOCK, 1), F32)

    for h in range(KV_HEADS):
        slab = (h // 2) * LANES
        k3 = jnp.concatenate([r[:, slab:slab + LANES] for r in (kp_ref, kc_ref, kn_ref)], axis=0).astype(F32)
        v3 = jnp.concatenate([r[:, slab:slab + LANES] for r in (vp_ref, vc_ref, vn_ref)], axis=0).astype(F32)
        k3r = pltpu.roll(k3, HEAD_DIM, 1)
        v3r = pltpu.roll(v3, HEAD_DIM, 1)
        if h % 2 == 0:
            ka, kb, va, vb = k3, k3r, v3, v3r
        else:
            ka, kb, va, vb = k3r, k3, v3r, v3
        kbd = jnp.concatenate([jnp.where(left_k, ka, 0.0), jnp.where(left_k, 0.0, kb)], axis=0).astype(BF16)
        vbd = jnp.concatenate(
            [jnp.concatenate([jnp.where(left_k, va, 0.0), ones_l], axis=1),
             jnp.concatenate([jnp.where(left_k, 0.0, vb), ones_r], axis=1)], axis=0).astype(BF16)
        for p in range(ATTN_HEADS // KV_HEADS // 2):
            g = h * 4 + p
            q2 = q_ref[:, g * LANES:(g + 1) * LANES]
            s2 = lax.dot_general(q2, kbd, (((1,), (1,)), ((), ())), preferred_element_type=F32)
            sa = jnp.where(valid, s2[:, :band] * scale, NEG_INF)
            sb = jnp.where(valid, s2[:, band:] * scale, NEG_INF)
            sink_a = sink_ref[2 * g]
            sink_b = sink_ref[2 * g + 1]
            ma = jnp.maximum(jnp.max(sa, axis=-1, keepdims=True), sink_a)
            mb = jnp.maximum(jnp.max(sb, axis=-1, keepdims=True), sink_b)
            p2 = jnp.concatenate([jnp.exp(sa - ma), jnp.exp(sb - mb)], axis=1).astype(BF16)
            o2 = jnp.dot(p2, vbd, preferred_element_type=F32)
            den = o2[:, LANES:] + jnp.where(left_q, jnp.exp(sink_a - ma), jnp.exp(sink_b - mb))
            out = o2[:, :LANES] / den
            ssq = ssq + jnp.sum(out * out, axis=-1, keepdims=True)
            obuf_ref[:, g * LANES:(g + 1) * LANES] = out

    r = lax.rsqrt(ssq * (1.0 / ATTN_WIDTH) + EPS)
    o_ref[...] = (obuf_ref[...] * r * gg_ref[...]).astype(o_ref.dtype)


def _window_attn(proj3, sink, gg):
    b, s, _ = proj3.shape
    nb = s // BLOCK
    kcol, vcol = COL_K // KV_WIDTH, COL_V // KV_WIDTH

    def band_spec(col, shift):
        return pl.BlockSpec((None, BLOCK, KV_WIDTH),
                            lambda bi, n: (bi, jnp.clip(n + shift, 0, nb - 1), col))

    return pl.pallas_call(
        functools.partial(_window_attn_kernel, nb=nb),
        out_shape=jax.ShapeDtypeStruct((b, s, ATTN_WIDTH), BF16),
        grid=(b, nb),
        in_specs=[
            pl.BlockSpec(memory_space=pltpu.SMEM),
            pl.BlockSpec((None, BLOCK, ATTN_WIDTH), lambda bi, n: (bi, n, 0)),
            band_spec(kcol, -1), band_spec(kcol, 0), band_spec(kcol, 1),
            band_spec(vcol, -1), band_spec(vcol, 0), band_spec(vcol, 1),
            pl.BlockSpec((1, ATTN_WIDTH), lambda bi, n: (0, 0)),
        ],
        out_specs=pl.BlockSpec((None, BLOCK, ATTN_WIDTH), lambda bi, n: (bi, n, 0)),
        scratch_shapes=[pltpu.VMEM((BLOCK, ATTN_WIDTH), F32)],
        compiler_params=_params(("parallel", "parallel"), 32),
        name="window_attn",
    )(sink, proj3, proj3, proj3, proj3, proj3, proj3, proj3, gg.reshape(1, ATTN_WIDTH))


def _sgu_kernel(u_ref, v_ref, ws_ref, bs_ref, gs_ref, gg_ref, o_ref, *, chunks):
    for c in range(chunks):
        rows = slice(c * CHUNK, (c + 1) * CHUNK)
        ug = jax.nn.gelu(u_ref[rows, :].astype(F32))
        vg = jax.nn.gelu(v_ref[rows, :].astype(F32))
        mixed = []
        for h in range(SGU_GROUPS):
            cols = slice(h * SGU_GROUP_DIM, (h + 1) * SGU_GROUP_DIM)
            vh = vg[:, cols]
            r = lax.rsqrt(jnp.mean(vh * vh, axis=-1, keepdims=True) + EPS)
            vn = (vh * r * gs_ref[:, cols]).astype(BF16)
            mixed.append(jnp.dot(ws_ref[h], vn, preferred_element_type=F32))
        o = ug * (jnp.concatenate(mixed, axis=1) + bs_ref[...])
        r = lax.rsqrt(jnp.mean(o * o, axis=-1, keepdims=True) + EPS)
        o_ref[rows, :] = (o * r * gg_ref[...]).astype(o_ref.dtype)


def _sgu(proj3, ws, bs_full, gs, gg, *, rows_pref=512):
    b, s, _ = proj3.shape
    tr = _tile(s, rows_pref)
    const = lambda bi, n: (0, 0)
    return pl.pallas_call(
        functools.partial(_sgu_kernel, chunks=tr // CHUNK),
        out_shape=jax.ShapeDtypeStruct((b, s, SGU_WIDTH), BF16),
        grid=(b, s // tr),
        in_specs=[
            pl.BlockSpec((None, tr, SGU_WIDTH), lambda bi, n: (bi, n, COL_U // SGU_WIDTH)),
            pl.BlockSpec((None, tr, SGU_WIDTH), lambda bi, n: (bi, n, COL_VS // SGU_WIDTH)),
            pl.BlockSpec((SGU_GROUPS, CHUNK, CHUNK), lambda bi, n: (0, 0, 0)),
            pl.BlockSpec((CHUNK, SGU_WIDTH), const),
            pl.BlockSpec((1, SGU_WIDTH), const),
            pl.BlockSpec((1, SGU_WIDTH), const),
        ],
        out_specs=pl.BlockSpec((None, tr, SGU_WIDTH), lambda bi, n: (bi, n, 0)),
        compiler_params=_params(("parallel", "parallel"), 32),
        name="sgu",
    )(proj3, proj3, ws, bs_full, gs.reshape(1, SGU_WIDTH), gg.reshape(1, SGU_WIDTH))


def _mem_xattn_kernel(q_ref, kv_ref, gg_ref, o_ref):
    scale = XATTN_HEAD_DIM ** -0.5
    outs = []
    ssq = None
    for h in range(XATTN_HEADS):
        cols = slice(h * XATTN_HEAD_DIM, (h + 1) * XATTN_HEAD_DIM)
        vcols = slice(XATTN_WIDTH + h * XATTN_HEAD_DIM, XATTN_WIDTH + (h + 1) * XATTN_HEAD_DIM)
        s = lax.dot_general(q_ref[:, cols], kv_ref[:, cols], (((1,), (1,)), ((), ())),
                            preferred_element_type=F32) * scale
        m = jnp.max(s, axis=-1, keepdims=True)
        p = jnp.exp(s - m)
        l = jnp.sum(p, axis=-1, keepdims=True)
        o = jnp.dot(p.astype(BF16), kv_ref[:, vcols], preferred_element_type=F32) / l
        sq = jnp.sum(o * o, axis=-1, keepdims=True)
        ssq = sq if ssq is None else ssq + sq
        outs.append(o)
    r = lax.rsqrt(ssq * (1.0 / XATTN_WIDTH) + EPS)
    for h in range(XATTN_HEADS):
        cols = slice(h * XATTN_HEAD_DIM, (h + 1) * XATTN_HEAD_DIM)
        o_ref[:, cols] = (outs[h] * r * gg_ref[:, cols]).astype(o_ref.dtype)


def _mem_xattn(proj3, kv3, gg, *, rows_pref=512):
    b, s, _ = proj3.shape
    tq = _tile(s, rows_pref)
    return pl.pallas_call(
        _mem_xattn_kernel,
        out_shape=jax.ShapeDtypeStruct((b, s, XATTN_WIDTH), BF16),
        grid=(b, s // tq),
        in_specs=[
            pl.BlockSpec((None, tq, XATTN_WIDTH), lambda bi, n: (bi, n, COL_XQ // XATTN_WIDTH)),
            pl.BlockSpec((None, N_MEM, 2 * XATTN_WIDTH), lambda bi, n: (bi, 0, 0)),
            pl.BlockSpec((1, XATTN_WIDTH), lambda bi, n: (0, 0)),
        ],
        out_specs=pl.BlockSpec((None, tq, XATTN_WIDTH), lambda bi, n: (bi, n, 0)),
        compiler_params=_params(("parallel", "parallel"), 32),
        name="mem_xattn",
    )(proj3, kv3, gg.reshape(1, XATTN_WIDTH))


def _out_proj_kernel(oa_ref, os_ref, ox_ref, w_ref, x_ref, h_ref):
    a1, a2 = ATTN_WIDTH, ATTN_WIDTH + SGU_WIDTH
    acc = jnp.dot(oa_ref[...], w_ref[:a1, :], preferred_element_type=F32)
    acc += jnp.dot(os_ref[...], w_ref[a1:a2, :], preferred_element_type=F32)
    acc += jnp.dot(ox_ref[...], w_ref[a2:, :], preferred_element_type=F32)
    h_ref[...] = x_ref[...] + acc


def _out_proj(oa, osg, ox, w, x, *, tm_pref=512, tn=1024):
    t, d = x.shape
    tm = _tile(t, tm_pref)
    return pl.pallas_call(
        _out_proj_kernel,
        out_shape=jax.ShapeDtypeStruct((t, d), F32),
        grid=(t // tm, d // tn),
        in_specs=[
            pl.BlockSpec((tm, ATTN_WIDTH), lambda i, j: (i, 0)),
            pl.BlockSpec((tm, SGU_WIDTH), lambda i, j: (i, 0)),
            pl.BlockSpec((tm, XATTN_WIDTH), lambda i, j: (i, 0)),
            pl.BlockSpec((d, tn), lambda i, j: (0, j)),
            pl.BlockSpec((tm, tn), lambda i, j: (i, j)),
        ],
        out_specs=pl.BlockSpec((tm, tn), lambda i, j: (i, j)),
        compiler_params=_params(("parallel", "arbitrary"), 48),
        name="out_proj",
    )(oa, osg, ox, w, x)


SLAB_ROWS = D_MODEL // LANES
SLAB_PITCH = SLAB_ROWS + 4


def _router_kernel(h1_ref, h2_ref, g_ref, wr_ref, br_ref, xn_ref, meta_ref, gate_ref, cnt_ref, run_ref,
                   *, nb1, tm):
    i = pl.program_id(0)

    @pl.when(i == 0)
    def _():
        run_ref[...] = jnp.zeros_like(run_ref)

    def body(h_ref):
        x = h_ref[...]
        ms = jnp.mean(x * x, axis=-1, keepdims=True)
        xn = x * lax.rsqrt(ms + EPS) * g_ref[...]
        for c in range(SLAB_ROWS):
            xn_ref[pl.ds(c, tm, stride=SLAB_PITCH), :] = xn[:, c * LANES:(c + 1) * LANES]
        for c in range(SLAB_ROWS, SLAB_PITCH):
            xn_ref[pl.ds(c, tm, stride=SLAB_PITCH), :] = jnp.zeros((tm, LANES), F32)
        logits = jnp.dot(xn.astype(BF16), wr_ref[...], preferred_element_type=F32) + br_ref[...]
        lane_e = lax.broadcasted_iota(jnp.int32, (tm, N_EXPERTS), 1)
        lane_o = lax.broadcasted_iota(jnp.int32, (tm, LANES), 1)
        row = lax.broadcasted_iota(jnp.int32, (tm, tm), 0)
        col = lax.broadcasted_iota(jnp.int32, (tm, tm), 1)
        earlier = jnp.where(col < row, 1.0, 0.0).astype(BF16)
        vals, idxs = [], []
        l = logits
        for _ in range(TOP_K):
            m = jnp.max(l, axis=-1, keepdims=True)
            e = jnp.min(jnp.where(l == m, lane_e, N_EXPERTS), axis=-1, keepdims=True)
            vals.append(m)
            idxs.append(e)
            l = jnp.where(lane_e == e, -jnp.inf, l)
        es = [jnp.exp(v - vals[0]) for v in vals]
        den = es[0] + es[1] + es[2] + es[3]
        base = run_ref[...]
        meta = jnp.zeros((tm, LANES), jnp.int32)
        gate_out = jnp.zeros((tm, LANES), F32)
        for k in range(TOP_K):
            onehot = jnp.where(lane_e == idxs[k], 1.0, 0.0)
            before = jnp.dot(earlier, onehot.astype(BF16), preferred_element_type=F32)
            rank = jnp.sum(onehot * (before + base), axis=-1, keepdims=True)
            base = base + jnp.sum(onehot, axis=0, keepdims=True)
            meta = jnp.where(lane_o == k, idxs[k], meta)
            meta = jnp.where(lane_o == TOP_K + k, rank.astype(jnp.int32), meta)
            gate_out = jnp.where(lane_o == k, es[k] / den, gate_out)
        run_ref[...] = base
        cnt_ref[...] = base
        meta_ref[...] = meta
        gate_ref[...] = gate_out

    @pl.when(i < nb1)
    def _():
        body(h1_ref)

    @pl.when(i >= nb1)
    def _():
        body(h2_ref)


def _router(h1, h2, g, wr, br, *, tm):
    t1, d = h1.shape
    t2 = h2.shape[0]
    nb1, nb2 = t1 // tm, t2 // tm
    t_all = t1 + t2
    const = lambda i: (0, 0)
    return pl.pallas_call(
        functools.partial(_router_kernel, nb1=nb1, tm=tm),
        out_shape=(jax.ShapeDtypeStruct((t_all * SLAB_PITCH, LANES), F32),
                   jax.ShapeDtypeStruct((t_all, LANES), jnp.int32),
                   jax.ShapeDtypeStruct((t_all, LANES), F32),
                   jax.ShapeDtypeStruct((1, N_EXPERTS), F32)),
        grid=(nb1 + nb2,),
        in_specs=[
            pl.BlockSpec((tm, d), lambda i: (jnp.minimum(i, nb1 - 1), 0)),
            pl.BlockSpec((tm, d), lambda i: (jnp.maximum(i - nb1, 0), 0)),
            pl.BlockSpec((1, d), const),
            pl.BlockSpec((d, N_EXPERTS), const),
            pl.BlockSpec((1, N_EXPERTS), const),
        ],
        out_specs=(pl.BlockSpec((tm * SLAB_PITCH, LANES), lambda i: (i, 0)),
                   pl.BlockSpec((tm, LANES), lambda i: (i, 0)),
                   pl.BlockSpec((tm, LANES), lambda i: (i, 0)),
                   pl.BlockSpec((1, N_EXPERTS), const)),
        scratch_shapes=[pltpu.VMEM((1, N_EXPERTS), F32)],
        compiler_params=_params(("arbitrary",), 48),
        name="router",
    )(h1, h2, g.reshape(1, d), wr, br.reshape(1, N_EXPERTS))


def _moe_kernel(nt_ref, te_ref, tok_ref, tokn_ref, dstp_ref, xn_hbm, wgu_ref, bgu_ref, wd_ref, bd_ref,
                y_hbm, xs0, xs1, yb0, yb1, xb, gsem, ssem, *, tm, spare0):
    del te_ref
    i = pl.program_id(0)
    nt = nt_ref[0]
    rows = range(tm)

    def gather(idx_ref, r, xs, s):
        return pltpu.make_async_copy(xn_hbm.at[pl.ds(idx_ref[0, 0, r] * SLAB_PITCH, SLAB_ROWS), :],
                                     xs.at[pl.ds(r * SLAB_PITCH, SLAB_ROWS), :], gsem.at[s])

    def scatter(r, yb, s):
        return pltpu.make_async_copy(yb.at[pl.ds(r, 1), :], y_hbm.at[pl.ds(dstp_ref[0, 0, r], 1), :], ssem.at[s])

    def step(s, xs, xs_next, yb, yb_prev):
        for r in rows:
            gather(tok_ref, r, xs, s).wait()

        @pl.when(i >= 1)
        def _():
            for r in rows:
                scatter(r, yb, s).wait()

        @pl.when(nt > 0)
        def _():
            for r in rows:
                gather(tokn_ref, r, xs_next, 1 - s).start()
                scatter(r, yb_prev, 1 - s).start()
        for c in range(SLAB_ROWS):
            xb[:, c * LANES:(c + 1) * LANES] = xs[pl.ds(c, tm, stride=SLAB_PITCH), :].astype(BF16)
        gu = jnp.dot(xb[...], wgu_ref[...], preferred_element_type=F32) + bgu_ref[...]
        gate = jnp.minimum(gu[:, :D_EXPERT], SWIGLU_LIMIT)
        up = jnp.clip(gu[:, D_EXPERT:], -SWIGLU_LIMIT, SWIGLU_LIMIT)
        hid = (up + 1.0) * (gate * jax.nn.sigmoid(SWIGLU_ALPHA * gate))
        yb[...] = jnp.dot(hid.astype(BF16), wd_ref[...], preferred_element_type=F32) + bd_ref[...]

    @pl.when(i == 0)
    def _():
        yb1[...] = jnp.zeros_like(yb1)
        for r in rows:
            gather(tok_ref, r, xs0, 0).start()

    @pl.when(jnp.logical_and(i < nt, i % 2 == 0))
    def _():
        step(0, xs0, xs1, yb0, yb1)

    @pl.when(jnp.logical_and(i < nt, i % 2 == 1))
    def _():
        step(1, xs1, xs0, yb1, yb0)

    def drain(s, xs, yb, yb_prev):
        for r in rows:
            gather(tok_ref, r, xs, s).wait()
        for r in rows:
            scatter(r, yb, s).wait()
        for r in rows:
            scatter(r, yb_prev, 1 - s).start()
        for r in rows:
            scatter(r, yb_prev, 1 - s).wait()
        for q, ybq in enumerate((yb0, yb1)):
            fill = pltpu.make_async_copy(ybq, y_hbm.at[pl.ds(spare0 + q * tm, tm), :], ssem.at[q])
            fill.start()
            fill.wait()

    @pl.when(jnp.logical_and(i == nt, i % 2 == 0))
    def _():
        drain(0, xs0, yb0, yb1)

    @pl.when(jnp.logical_and(i == nt, i % 2 == 1))
    def _():
        drain(1, xs1, yb1, yb0)


def _moe(xn, nt, te, tok, dstp, wgu, bgu, wd, bd, *, tm, y_rows, spare0):
    ntmax = te.shape[0]
    d = D_MODEL
    row_spec = lambda shift, nrows: pl.BlockSpec(
        (1, 1, tm), lambda i, nt_, te_: (jnp.minimum(i + shift, nrows - 1), 0, 0), memory_space=pltpu.SMEM)
    expert = lambda i, nt_, te_: (te_[jnp.minimum(i, ntmax - 1)], 0, 0)
    grid_spec = pltpu.PrefetchScalarGridSpec(
        num_scalar_prefetch=2,
        grid=(ntmax + 1,),
        in_specs=[
            row_spec(0, ntmax), row_spec(1, ntmax), row_spec(0, ntmax + 1),
            pl.BlockSpec(memory_space=pl.ANY),
            pl.BlockSpec((None, d, 2 * D_EXPERT), expert),
            pl.BlockSpec((None, 1, 2 * D_EXPERT), expert),
            pl.BlockSpec((None, D_EXPERT, d), expert),
            pl.BlockSpec((None, 1, d), expert),
        ],
        out_specs=pl.BlockSpec(memory_space=pl.ANY),
        scratch_shapes=[
            pltpu.VMEM((tm * SLAB_PITCH, LANES), F32),
            pltpu.VMEM((tm * SLAB_PITCH, LANES), F32),
            pltpu.VMEM((tm, d), F32),
            pltpu.VMEM((tm, d), F32),
            pltpu.VMEM((tm, d), BF16),
            pltpu.SemaphoreType.DMA((2,)),
            pltpu.SemaphoreType.DMA((2,)),
        ],
    )
    return pl.pallas_call(
        functools.partial(_moe_kernel, tm=tm, spare0=spare0),
        out_shape=jax.ShapeDtypeStruct((y_rows, d), F32),
        grid_spec=grid_spec,
        compiler_params=_params(("arbitrary",), 58),
        name="moe_experts",
    )(nt, te, tok, tok, dstp, xn, wgu, bgu, wd, bd)


TOK_BITS = 20


def _route(meta, cnt, t_all, tm):
    a_all = TOP_K * t_all
    assert t_all < (1 << TOK_BITS)
    ntmax = (a_all + N_EXPERTS * (tm - 1)) // tm
    cnt = cnt.reshape(N_EXPERTS).astype(jnp.int32)
    tiles_e = (cnt + tm - 1) // tm
    tile_end = jnp.cumsum(tiles_e)
    tile_start = tile_end - tiles_e
    nt = tile_end[-1:]
    ti = jnp.arange(ntmax, dtype=jnp.int32)
    te = jnp.minimum(jnp.sum((tile_end[None, :] <= ti[:, None]).astype(jnp.int32), axis=1), N_EXPERTS - 1)
    idx, rank = meta[:, :TOP_K], meta[:, TOP_K:2 * TOP_K]
    slot = tile_start[idx] * tm + rank
    val = jnp.arange(t_all, dtype=jnp.int32)[:, None] + (jnp.arange(TOP_K, dtype=jnp.int32) << TOK_BITS)[None, :]
    table = jnp.full((ntmax * tm,), -1, jnp.int32).at[slot.reshape(-1)].set(
        val.reshape(-1), unique_indices=True, mode="promise_in_bounds").reshape(ntmax, tm)
    valid = table >= 0
    tok = jnp.where(valid, table & ((1 << TOK_BITS) - 1), 0)
    r = jnp.arange(tm, dtype=jnp.int32)[None, :]
    spare = a_all + (ti % 2)[:, None] * tm + r
    dst = jnp.where(valid, (table >> TOK_BITS) * t_all + tok, spare)
    dstp = jnp.concatenate([a_all + tm + r, dst], axis=0)
    return nt.astype(jnp.int32), te, tok.reshape(ntmax, 1, tm), dstp.reshape(ntmax + 1, 1, tm)


def _combine_kernel(h_ref, y0_ref, y1_ref, y2_ref, y3_ref, gate_ref, g_ref, o_ref):
    acc = h_ref[...]
    gates = gate_ref[...]
    for k, y_ref in enumerate((y0_ref, y1_ref, y2_ref, y3_ref)):
        acc = acc + gates[:, k:k + 1] * y_ref[...]
    ms = jnp.mean(acc * acc, axis=-1, keepdims=True)
    o_ref[...] = acc * lax.rsqrt(ms + EPS) * g_ref[...]


def _combine(h, y, gates, g, t_all, t0, *, tm):
    t, d = h.shape
    yspec = lambda k: pl.BlockSpec((tm, d), lambda i: ((k * t_all + t0) // tm + i, 0))
    return pl.pallas_call(
        _combine_kernel,
        out_shape=jax.ShapeDtypeStruct((t, d), F32),
        grid=(t // tm,),
        in_specs=[pl.BlockSpec((tm, d), lambda i: (i, 0)),
                  yspec(0), yspec(1), yspec(2), yspec(3),
                  pl.BlockSpec((tm, LANES), lambda i: (t0 // tm + i, 0)),
                  pl.BlockSpec((1, d), lambda i: (0, 0))],
        out_specs=pl.BlockSpec((tm, d), lambda i: (i, 0)),
        compiler_params=_params(("parallel",), 56),
        name="combine",
    )(h, y, y, y, y, gates, g.reshape(1, d))


def _mixer(x, mem, p):
    b, s, d = x.shape
    t = b * s
    x2 = x.reshape(t, d)
    proj = _norm_proj(x2, p["g_mix"], p["w_in"], _rope_tables(s), s)
    proj3 = proj.reshape(b, s, IN_WIDTH)
    kv = _norm_proj(mem.reshape(b * N_MEM, d), p["g_mem"], p["w_mem_kv"], tm_pref=256)
    gg = p["g_group"]
    oa = _window_attn(proj3, p["attn_sink"], gg[:ATTN_WIDTH])
    osg = _sgu(proj3, p["w_spatial"], p["b_spatial"], p["g_sgu"], gg[ATTN_WIDTH:ATTN_WIDTH + SGU_WIDTH])
    ox = _mem_xattn(proj3, kv.reshape(b, N_MEM, 2 * XATTN_WIDTH), gg[ATTN_WIDTH + SGU_WIDTH:])
    return _out_proj(oa.reshape(t, -1), osg.reshape(t, -1), ox.reshape(t, -1), p["w_out"], x2)


def kernel(x_prompt, x_sample, mem_prompt, mem_sample, g_mix, w_in, attn_sink, g_sgu, w_spatial, b_spatial,
           g_mem, w_mem_kv, g_group, w_out, g_ffn, w_router, b_router, w_gate_up, b_gate_up, w_down, b_down,
           g_final):
    assert g_mix.shape[0] == 1
    q, k, v, u, vs, xq = jnp.split(w_in[0], [2048, 2304, 2560, 3584, 4608], axis=1)
    p = {
        "g_mix": g_mix[0],
        "w_in": jnp.concatenate([q, u, vs, xq, k, v], axis=1).astype(BF16),
        "attn_sink": attn_sink[0],
        "g_sgu": g_sgu[0],
        "w_spatial": w_spatial[0].astype(BF16),
        "b_spatial": jnp.repeat(b_spatial[0].T, SGU_GROUP_DIM, axis=1),
        "g_mem": g_mem[0],
        "w_mem_kv": w_mem_kv[0].astype(BF16),
        "g_group": g_group[0],
        "w_out": w_out[0].astype(BF16),
    }
    h1 = _mixer(x_prompt, mem_prompt, p)
    h2 = _mixer(x_sample, mem_sample, p)
    t1, t2 = h1.shape[0], h2.shape[0]
    t_all = t1 + t2
    tm = 256
    assert t1 % tm == 0 and t2 % tm == 0
    xn, meta, gates, cnt = _router(h1, h2, g_ffn[0], w_router[0].astype(BF16), b_router[0], tm=tm)
    nt, te, tok, dstp = _route(meta, cnt, t_all, tm)
    spare0 = TOP_K * t_all
    y = _moe(xn, nt, te, tok, dstp,
             w_gate_up[0].astype(BF16), b_gate_up[0][:, None, :], w_down[0].astype(BF16), b_down[0][:, None, :],
             tm=tm, y_rows=spare0 + 2 * tm, spare0=spare0)
    out1 = _combine(h1, y, gates, g_final, t_all, 0, tm=tm).reshape(x_prompt.shape)
    out2 = _combine(h2, y, gates, g_final, t_all, t1, tm=tm).reshape(x_sample.shape)
    return out1, out2
```

```python
import functools

import jax
import jax.numpy as jnp
import numpy as np
from jax import lax
from jax.experimental import pallas as pl
from jax.experimental.pallas import tpu as pltpu

D_MODEL = 4096
N_MEM = 256
HEAD_DIM = 64
ATTN_WIDTH = 2048
ATTN_HEADS = 32
KV_HEADS = 4
KV_WIDTH = 256
WINDOW = 128
BLOCK = 128
ROPE_DIM = 16
ROPE_THETA = 500000.0
SGU_WIDTH = 1024
SGU_GROUPS = 4
SGU_GROUP_DIM = 256
CHUNK = 128
XATTN_WIDTH = 1024
XATTN_HEADS = 4
XATTN_HEAD_DIM = 256
IN_WIDTH = 5632
N_EXPERTS = 32
TOP_K = 4
D_EXPERT = 512
SWIGLU_LIMIT = 7.0
SWIGLU_ALPHA = 1.702
EPS = 1e-5
NEG_INF = -1e30

LANES = 128
COL_Q, COL_U, COL_VS, COL_XQ, COL_K, COL_V = 0, 2048, 3072, 4096, 5120, 5376

F32 = jnp.float32
BF16 = jnp.bfloat16
MIB = 1024 * 1024


def _params(semantics, vmem_mib):
    return pltpu.CompilerParams(dimension_semantics=semantics, vmem_limit_bytes=vmem_mib * MIB)


def _tile(n, pref):
    t = min(n, pref)
    while n % t or t % LANES:
        t -= LANES
    assert t > 0
    return t


def _rope_group(a, c, s1, s2):
    return a * c + pltpu.roll(a, LANES - ROPE_DIM // 2, 1) * s1 + pltpu.roll(a, ROPE_DIM // 2, 1) * s2


PROJ_TM, PROJ_TN, OUT_TN = 512, 512, 1024
CAST_BLOCK_BYTES = 2 * MIB


def _cast_rows(rows, cols, steps):
    nblk = 1 << (steps.bit_length() - 1)
    while rows % nblk:
        nblk //= 2
    rb = rows // nblk
    return rb if rb * cols * 4 <= CAST_BLOCK_BYTES and rb % 16 == 0 else None


def _cast_spec(nblk, rb, cols, steps_per_row):
    return pl.BlockSpec((rb, cols), lambda i, j: (jnp.minimum(i * steps_per_row + j, nblk - 1), 0))


def _norm_proj_kernel(*refs, rope, tn, cast):
    if cast:
        *ins, cast_in_ref, o_ref, cast_out_ref, xn_ref = refs
        refs = (*ins, o_ref, xn_ref)
        cast_out_ref[...] = cast_in_ref[...].astype(BF16)
    if rope:
        x_ref, g_ref, w_ref, c_ref, s1_ref, s2_ref, o_ref, xn_ref = refs
    else:
        x_ref, g_ref, w_ref, o_ref, xn_ref = refs
    j = pl.program_id(1)

    @pl.when(j == 0)
    def _():
        x = x_ref[...]
        ms = jnp.mean(x * x, axis=-1, keepdims=True)
        xn_ref[...] = (x * lax.rsqrt(ms + EPS) * g_ref[...]).astype(BF16)

    acc = jnp.dot(xn_ref[...], w_ref[...], preferred_element_type=F32)
    if not rope:
        o_ref[...] = acc.astype(o_ref.dtype)
        return

    ngroups = tn // LANES
    q_tiles = ATTN_WIDTH // tn
    k_tile = COL_K // tn
    k_groups = KV_WIDTH // LANES

    def store(n_rope):
        c, s1, s2 = c_ref[...], s1_ref[...], s2_ref[...]
        for gidx in range(ngroups):
            a = acc[:, gidx * LANES:(gidx + 1) * LANES]
            if gidx < n_rope:
                a = _rope_group(a, c, s1, s2)
            o_ref[:, gidx * LANES:(gidx + 1) * LANES] = a.astype(o_ref.dtype)

    @pl.when(j < q_tiles)
    def _():
        store(ngroups)

    @pl.when(j == k_tile)
    def _():
        store(k_groups)

    @pl.when(jnp.logical_and(j >= q_tiles, j != k_tile))
    def _():
        store(0)


def _norm_proj(x, g, w, rope_tabs=None, seq=None, *, tm_pref=PROJ_TM, tn=PROJ_TN, cast_src=None, cast_rb=None):
    t, d = x.shape
    n = w.shape[1]
    tm = _tile(t if seq is None else seq, tm_pref)
    rope = rope_tabs is not None
    cast = cast_src is not None
    in_specs = [
        pl.BlockSpec((tm, d), lambda i, j: (i, 0)),
        pl.BlockSpec((1, d), lambda i, j: (0, 0)),
        pl.BlockSpec((d, tn), lambda i, j: (0, j)),
    ]
    args = [x, g.reshape(1, d), w]
    if rope:
        assert COL_K % tn == 0 and ATTN_WIDTH % tn == 0 and tn >= KV_WIDTH
        sblocks = seq // tm
        tab_spec = pl.BlockSpec((tm, LANES), lambda i, j: (i % sblocks, 0))
        in_specs += [tab_spec, tab_spec, tab_spec]
        args += list(rope_tabs)
    out_shape = jax.ShapeDtypeStruct((t, n), BF16)
    out_specs = pl.BlockSpec((tm, tn), lambda i, j: (i, j))
    if cast:
        rows, cols = cast_src.shape
        nblk = rows // cast_rb
        assert rows % cast_rb == 0 and nblk <= (t // tm) * (n // tn)
        spec = _cast_spec(nblk, cast_rb, cols, n // tn)
        in_specs.append(spec)
        args.append(cast_src)
        out_shape = (out_shape, jax.ShapeDtypeStruct((rows, cols), BF16))
        out_specs = (out_specs, spec)
    return pl.pallas_call(
        functools.partial(_norm_proj_kernel, rope=rope, tn=tn, cast=cast),
        out_shape=out_shape,
        grid=(t // tm, n // tn),
        in_specs=in_specs,
        out_specs=out_specs,
        scratch_shapes=[pltpu.VMEM((tm, d), BF16)],
        compiler_params=_params(("arbitrary", "arbitrary") if cast else ("parallel", "arbitrary"), 48),
        name="norm_proj_rope" if rope else "norm_proj",
    )(*args)


def _rope_tables(seq):
    half = ROPE_DIM // 2
    inv_freq = ROPE_THETA ** (-jnp.arange(0, ROPE_DIM, 2, dtype=F32) / ROPE_DIM)
    ang = jnp.arange(seq, dtype=F32)[:, None] * inv_freq[None, :]
    cos, sin = jnp.cos(ang), jnp.sin(ang)
    dim = np.arange(LANES) % HEAD_DIM
    sel = dim % half
    cos_l, sin_l = cos[:, sel], sin[:, sel]
    c = jnp.where(dim < ROPE_DIM, cos_l, 1.0)
    s1 = jnp.where(dim < half, -sin_l, 0.0)
    s2 = jnp.where((dim >= half) & (dim < ROPE_DIM), sin_l, 0.0)
    return c.astype(F32), s1.astype(F32), s2.astype(F32)


def _window_attn_kernel(sink_ref, q_ref, kp_ref, kc_ref, kn_ref, vp_ref, vc_ref, vn_ref, gg_ref,
                        o_ref, obuf_ref, *, nb):
    n = pl.program_id(1)
    band = 3 * BLOCK
    qi = lax.broadcasted_iota(jnp.int32, (BLOCK, band), 0)
    ki = lax.broadcasted_iota(jnp.int32, (BLOCK, band), 1)
    rel = ki - BLOCK - qi
    lo = jnp.where(n == 0, BLOCK, 0)
    hi = jnp.where(n == nb - 1, 2 * BLOCK, band)
    valid = (jnp.abs(rel) <= WINDOW) & (ki >= lo) & (ki < hi)
    lane_k = lax.broadcasted_iota(jnp.int32, (band, LANES), 1)
    left_k = lane_k < HEAD_DIM
    lane_q = lax.broadcasted_iota(jnp.int32, (BLOCK, LANES), 1)
    left_q = lane_q < HEAD_DIM
    ones_l = jnp.where(left_k, 1.0, 0.0).astype(F32)
    ones_r = 1.0 - ones_l
    scale = HEAD_DIM ** -0.5
    ssq = jnp.zeros((BLOCK, 1), F32)

    for h in range(KV_HEADS):
        slab = (h // 2) * LANES
        k3 = jnp.concatenate([r[:, slab:slab + LANES] for r in (kp_ref, kc_ref, kn_ref)], axis=0).astype(F32)
        v3 = jnp.concatenate([r[:, slab:slab + LANES] for r in (vp_ref, vc_ref, vn_ref)], axis=0).astype(F32)
        k3r = pltpu.roll(k3, HEAD_DIM, 1)
        v3r = pltpu.roll(v3, HEAD_DIM, 1)
        if h % 2 == 0:
            ka, kb, va, vb = k3, k3r, v3, v3r
        else:
            ka, kb, va, vb = k3r, k3, v3r, v3
        kbd = jnp.concatenate([jnp.where(left_k, ka, 0.0), jnp.where(left_k, 0.0, kb)], axis=0).astype(BF16)
        vbd = jnp.concatenate(
            [jnp.concatenate([jnp.where(left_k, va, 0.0), ones_l], axis=1),
             jnp.concatenate([jnp.where(left_k, 0.0, vb), ones_r], axis=1)], axis=0).astype(BF16)
        for p in range(ATTN_HEADS // KV_HEADS // 2):
            g = h * 4 + p
            q2 = q_ref[:, g * LANES:(g + 1) * LANES]
            s2 = lax.dot_general(q2, kbd, (((1,), (1,)), ((), ())), preferred_element_type=F32)
            sa = jnp.where(valid, s2[:, :band] * scale, NEG_INF)
            sb = jnp.where(valid, s2[:, band:] * scale, NEG_INF)
            sink_a = sink_ref[2 * g]
            sink_b = sink_ref[2 * g + 1]
            ma = jnp.maximum(jnp.max(sa, axis=-1, keepdims=True), sink_a)
            mb = jnp.maximum(jnp.max(sb, axis=-1, keepdims=True), sink_b)
            p2 = jnp.concatenate([jnp.exp(sa - ma), jnp.exp(sb - mb)], axis=1).astype(BF16)
            o2 = jnp.dot(p2, vbd, preferred_element_type=F32)
            den = o2[:, LANES:] + jnp.where(left_q, jnp.exp(sink_a - ma), jnp.exp(sink_b - mb))
            out = o2[:, :LANES] / den
            ssq = ssq + jnp.sum(out * out, axis=-1, keepdims=True)
            obuf_ref[:, g * LANES:(g + 1) * LANES] = out

    r = lax.rsqrt(ssq * (1.0 / ATTN_WIDTH) + EPS)
    o_ref[...] = (obuf_ref[...] * r * gg_ref[...]).astype(o_ref.dtype)


def _window_attn(proj3, sink, gg):
    b, s, _ = proj3.shape
    nb = s // BLOCK
    kcol, vcol = COL_K // KV_WIDTH, COL_V // KV_WIDTH

    def band_spec(col, shift):
        return pl.BlockSpec((None, BLOCK, KV_WIDTH),
                            lambda bi, n: (bi, jnp.clip(n + shift, 0, nb - 1), col))

    return pl.pallas_call(
        functools.partial(_window_attn_kernel, nb=nb),
        out_shape=jax.ShapeDtypeStruct((b, s, ATTN_WIDTH), BF16),
        grid=(b, nb),
        in_specs=[
            pl.BlockSpec(memory_space=pltpu.SMEM),
            pl.BlockSpec((None, BLOCK, ATTN_WIDTH), lambda bi, n: (bi, n, 0)),
            band_spec(kcol, -1), band_spec(kcol, 0), band_spec(kcol, 1),
            band_spec(vcol, -1), band_spec(vcol, 0), band_spec(vcol, 1),
            pl.BlockSpec((1, ATTN_WIDTH), lambda bi, n: (0, 0)),
        ],
        out_specs=pl.BlockSpec((None, BLOCK, ATTN_WIDTH), lambda bi, n: (bi, n, 0)),
        scratch_shapes=[pltpu.VMEM((BLOCK, ATTN_WIDTH), F32)],
        compiler_params=_params(("parallel", "parallel"), 32),
        name="window_attn",
    )(sink, proj3, proj3, proj3, proj3, proj3, proj3, proj3, gg.reshape(1, ATTN_WIDTH))


def _sgu_kernel(u_ref, v_ref, ws_ref, bs_ref, gs_ref, gg_ref, o_ref, *, chunks):
    for c in range(chunks):
        rows = slice(c * CHUNK, (c + 1) * CHUNK)
        ug = jax.nn.gelu(u_ref[rows, :].astype(F32))
        vg = jax.nn.gelu(v_ref[rows, :].astype(F32))
        mixed = []
        for h in range(SGU_GROUPS):
            cols = slice(h * SGU_GROUP_DIM, (h + 1) * SGU_GROUP_DIM)
            vh = vg[:, cols]
            r = lax.rsqrt(jnp.mean(vh * vh, axis=-1, keepdims=True) + EPS)
            vn = (vh * r * gs_ref[:, cols]).astype(BF16)
            mixed.append(jnp.dot(ws_ref[h], vn, preferred_element_type=F32))
        o = ug * (jnp.concatenate(mixed, axis=1) + bs_ref[...])
        r = lax.rsqrt(jnp.mean(o * o, axis=-1, keepdims=True) + EPS)
        o_ref[rows, :] = (o * r * gg_ref[...]).astype(o_ref.dtype)


def _sgu(proj3, ws, bs_full, gs, gg, *, rows_pref=512):
    b, s, _ = proj3.shape
    tr = _tile(s, rows_pref)
    const = lambda bi, n: (0, 0)
    return pl.pallas_call(
        functools.partial(_sgu_kernel, chunks=tr // CHUNK),
        out_shape=jax.ShapeDtypeStruct((b, s, SGU_WIDTH), BF16),
        grid=(b, s // tr),
        in_specs=[
            pl.BlockSpec((None, tr, SGU_WIDTH), lambda bi, n: (bi, n, COL_U // SGU_WIDTH)),
            pl.BlockSpec((None, tr, SGU_WIDTH), lambda bi, n: (bi, n, COL_VS // SGU_WIDTH)),
            pl.BlockSpec((SGU_GROUPS, CHUNK, CHUNK), lambda bi, n: (0, 0, 0)),
            pl.BlockSpec((CHUNK, SGU_WIDTH), const),
            pl.BlockSpec((1, SGU_WIDTH), const),
            pl.BlockSpec((1, SGU_WIDTH), const),
        ],
        out_specs=pl.BlockSpec((None, tr, SGU_WIDTH), lambda bi, n: (bi, n, 0)),
        compiler_params=_params(("parallel", "parallel"), 32),
        name="sgu",
    )(proj3, proj3, ws, bs_full, gs.reshape(1, SGU_WIDTH), gg.reshape(1, SGU_WIDTH))


def _mem_xattn_kernel(q_ref, kv_ref, gg_ref, o_ref):
    scale = XATTN_HEAD_DIM ** -0.5
    outs = []
    ssq = None
    for h in range(XATTN_HEADS):
        cols = slice(h * XATTN_HEAD_DIM, (h + 1) * XATTN_HEAD_DIM)
        vcols = slice(XATTN_WIDTH + h * XATTN_HEAD_DIM, XATTN_WIDTH + (h + 1) * XATTN_HEAD_DIM)
        s = lax.dot_general(q_ref[:, cols], kv_ref[:, cols], (((1,), (1,)), ((), ())),
                            preferred_element_type=F32) * scale
        m = jnp.max(s, axis=-1, keepdims=True)
        p = jnp.exp(s - m)
        l = jnp.sum(p, axis=-1, keepdims=True)
        o = jnp.dot(p.astype(BF16), kv_ref[:, vcols], preferred_element_type=F32) / l
        sq = jnp.sum(o * o, axis=-1, keepdims=True)
        ssq = sq if ssq is None else ssq + sq
        outs.append(o)
    r = lax.rsqrt(ssq * (1.0 / XATTN_WIDTH) + EPS)
    for h in range(XATTN_HEADS):
        cols = slice(h * XATTN_HEAD_DIM, (h + 1) * XATTN_HEAD_DIM)
        o_ref[:, cols] = (outs[h] * r * gg_ref[:, cols]).astype(o_ref.dtype)


def _mem_xattn(proj3, kv3, gg, *, rows_pref=512):
    b, s, _ = proj3.shape
    tq = _tile(s, rows_pref)
    return pl.pallas_call(
        _mem_xattn_kernel,
        out_shape=jax.ShapeDtypeStruct((b, s, XATTN_WIDTH), BF16),
        grid=(b, s // tq),
        in_specs=[
            pl.BlockSpec((None, tq, XATTN_WIDTH), lambda bi, n: (bi, n, COL_XQ // XATTN_WIDTH)),
            pl.BlockSpec((None, N_MEM, 2 * XATTN_WIDTH), lambda bi, n: (bi, 0, 0)),
            pl.BlockSpec((1, XATTN_WIDTH), lambda bi, n: (0, 0)),
        ],
        out_specs=pl.BlockSpec((None, tq, XATTN_WIDTH), lambda bi, n: (bi, n, 0)),
        compiler_params=_params(("parallel", "parallel"), 32),
        name="mem_xattn",
    )(proj3, kv3, gg.reshape(1, XATTN_WIDTH))


def _out_proj_kernel(oa_ref, os_ref, ox_ref, w_ref, x_ref, *rest):
    if len(rest) == 3:
        cast_in_ref, h_ref, cast_out_ref = rest
        cast_out_ref[...] = cast_in_ref[...].astype(BF16)
    else:
        (h_ref,) = rest
    a1, a2 = ATTN_WIDTH, ATTN_WIDTH + SGU_WIDTH
    acc = jnp.dot(oa_ref[...], w_ref[:a1, :], preferred_element_type=F32)
    acc += jnp.dot(os_ref[...], w_ref[a1:a2, :], preferred_element_type=F32)
    acc += jnp.dot(ox_ref[...], w_ref[a2:, :], preferred_element_type=F32)
    h_ref[...] = x_ref[...] + acc


def _out_proj(oa, osg, ox, w, x, *, tm_pref=PROJ_TM, tn=OUT_TN, cast_src=None, cast_rb=None):
    t, d = x.shape
    tm = _tile(t, tm_pref)
    cast = cast_src is not None
    in_specs = [
        pl.BlockSpec((tm, ATTN_WIDTH), lambda i, j: (i, 0)),
        pl.BlockSpec((tm, SGU_WIDTH), lambda i, j: (i, 0)),
        pl.BlockSpec((tm, XATTN_WIDTH), lambda i, j: (i, 0)),
        pl.BlockSpec((d, tn), lambda i, j: (0, j)),
        pl.BlockSpec((tm, tn), lambda i, j: (i, j)),
    ]
    args = [oa, osg, ox, w, x]
    out_shape = jax.ShapeDtypeStruct((t, d), F32)
    out_specs = pl.BlockSpec((tm, tn), lambda i, j: (i, j))
    if cast:
        rows, cols = cast_src.shape
        nblk = rows // cast_rb
        assert rows % cast_rb == 0 and nblk <= (t // tm) * (d // tn)
        spec = _cast_spec(nblk, cast_rb, cols, d // tn)
        in_specs.append(spec)
        args.append(cast_src)
        out_shape = (out_shape, jax.ShapeDtypeStruct((rows, cols), BF16))
        out_specs = (out_specs, spec)
    return pl.pallas_call(
        _out_proj_kernel,
        out_shape=out_shape,
        grid=(t // tm, d // tn),
        in_specs=in_specs,
        out_specs=out_specs,
        compiler_params=_params(("arbitrary", "arbitrary") if cast else ("parallel", "arbitrary"), 48),
        name="out_proj",
    )(*args)


HALF = D_MODEL // 2
U32 = jnp.uint32
SLAB_ROWS = HALF // LANES
SLAB_PITCH = SLAB_ROWS + 4


def _pack_halves(x):
    return pltpu.pack_elementwise([x[:, :HALF], x[:, HALF:]], packed_dtype=BF16)


def _unpack_halves(w):
    return tuple(pltpu.unpack_elementwise(w, index=i, packed_dtype=BF16, unpacked_dtype=F32) for i in (0, 1))


def _router_kernel(h1_ref, h2_ref, g_ref, wr_ref, br_ref, xn_ref, meta_ref, gate_ref, cnt_ref, run_ref,
                   *, nb1, tm):
    i = pl.program_id(0)

    @pl.when(i == 0)
    def _():
        run_ref[...] = jnp.zeros_like(run_ref)

    def body(h_ref):
        x = h_ref[...]
        ms = jnp.mean(x * x, axis=-1, keepdims=True)
        xn = x * lax.rsqrt(ms + EPS) * g_ref[...]
        words = _pack_halves(xn)
        for c in range(SLAB_ROWS):
            xn_ref[pl.ds(c, tm, stride=SLAB_PITCH), :] = words[:, c * LANES:(c + 1) * LANES]
        for c in range(SLAB_ROWS, SLAB_PITCH):
            xn_ref[pl.ds(c, tm, stride=SLAB_PITCH), :] = jnp.zeros((tm, LANES), U32)
        logits = jnp.dot(xn.astype(BF16), wr_ref[...], preferred_element_type=F32) + br_ref[...]
        lane_e = lax.broadcasted_iota(jnp.int32, (tm, N_EXPERTS), 1)
        lane_o = lax.broadcasted_iota(jnp.int32, (tm, LANES), 1)
        row = lax.broadcasted_iota(jnp.int32, (tm, tm), 0)
        col = lax.broadcasted_iota(jnp.int32, (tm, tm), 1)
        earlier = jnp.where(col < row, 1.0, 0.0).astype(BF16)
        vals, idxs = [], []
        l = logits
        for _ in range(TOP_K):
            m = jnp.max(l, axis=-1, keepdims=True)
            e = jnp.min(jnp.where(l == m, lane_e, N_EXPERTS), axis=-1, keepdims=True)
            vals.append(m)
            idxs.append(e)
            l = jnp.where(lane_e == e, -jnp.inf, l)
        es = [jnp.exp(v - vals[0]) for v in vals]
        den = es[0] + es[1] + es[2] + es[3]
        base = run_ref[...]
        meta = jnp.zeros((tm, LANES), jnp.int32)
        gate_out = jnp.zeros((tm, LANES), F32)
        for k in range(TOP_K):
            onehot = jnp.where(lane_e == idxs[k], 1.0, 0.0)
            before = jnp.dot(earlier, onehot.astype(BF16), preferred_element_type=F32)
            rank = jnp.sum(onehot * (before + base), axis=-1, keepdims=True)
            base = base + jnp.sum(onehot, axis=0, keepdims=True)
            meta = jnp.where(lane_o == k, idxs[k], meta)
            meta = jnp.where(lane_o == TOP_K + k, rank.astype(jnp.int32), meta)
            gate_out = jnp.where(lane_o == k, es[k] / den, gate_out)
        run_ref[...] = base
        cnt_ref[...] = base
        meta_ref[...] = meta
        gate_ref[...] = gate_out

    @pl.when(i < nb1)
    def _():
        body(h1_ref)

    @pl.when(i >= nb1)
    def _():
        body(h2_ref)


def _router(h1, h2, g, wr, br, *, tm):
    t1, d = h1.shape
    t2 = h2.shape[0]
    nb1, nb2 = t1 // tm, t2 // tm
    t_all = t1 + t2
    const = lambda i: (0, 0)
    return pl.pallas_call(
        functools.partial(_router_kernel, nb1=nb1, tm=tm),
        out_shape=(jax.ShapeDtypeStruct((t_all * SLAB_PITCH, LANES), U32),
                   jax.ShapeDtypeStruct((t_all, LANES), jnp.int32),
                   jax.ShapeDtypeStruct((t_all, LANES), F32),
                   jax.ShapeDtypeStruct((1, N_EXPERTS), F32)),
        grid=(nb1 + nb2,),
        in_specs=[
            pl.BlockSpec((tm, d), lambda i: (jnp.minimum(i, nb1 - 1), 0)),
            pl.BlockSpec((tm, d), lambda i: (jnp.maximum(i - nb1, 0), 0)),
            pl.BlockSpec((1, d), const),
            pl.BlockSpec((d, N_EXPERTS), const),
            pl.BlockSpec((1, N_EXPERTS), const),
        ],
        out_specs=(pl.BlockSpec((tm * SLAB_PITCH, LANES), lambda i: (i, 0)),
                   pl.BlockSpec((tm, LANES), lambda i: (i, 0)),
                   pl.BlockSpec((tm, LANES), lambda i: (i, 0)),
                   pl.BlockSpec((1, N_EXPERTS), const)),
        scratch_shapes=[pltpu.VMEM((1, N_EXPERTS), F32)],
        compiler_params=_params(("arbitrary",), 48),
        name="router",
    )(h1, h2, g.reshape(1, d), wr, br.reshape(1, N_EXPERTS))


def _moe_kernel(nt_ref, te_ref, tok_ref, tokn_ref, dstp_ref, xn_hbm, wgu_ref, bgu_ref, wd_ref, bd_ref,
                y_hbm, xs0, xs1, yb0, yb1, xb, gsem, ssem, *, tm, spare0):
    del te_ref
    i = pl.program_id(0)
    nt = nt_ref[0]
    rows = range(tm)

    def gather(idx_ref, r, xs, s):
        return pltpu.make_async_copy(xn_hbm.at[pl.ds(idx_ref[0, 0, r] * SLAB_PITCH, SLAB_ROWS), :],
                                     xs.at[pl.ds(r * SLAB_PITCH, SLAB_ROWS), :], gsem.at[s])

    def scatter(r, yb, s):
        return pltpu.make_async_copy(yb.at[pl.ds(r, 1), :], y_hbm.at[pl.ds(dstp_ref[0, 0, r], 1), :], ssem.at[s])

    def step(s, xs, xs_next, yb, yb_prev):
        for r in rows:
            gather(tok_ref, r, xs, s).wait()

        @pl.when(i >= 1)
        def _():
            for r in rows:
                scatter(r, yb, s).wait()

        @pl.when(nt > 0)
        def _():
            for r in rows:
                gather(tokn_ref, r, xs_next, 1 - s).start()
                scatter(r, yb_prev, 1 - s).start()
        for c in range(SLAB_ROWS):
            lo, hi = _unpack_halves(xs[pl.ds(c, tm, stride=SLAB_PITCH), :])
            xb[:, c * LANES:(c + 1) * LANES] = lo.astype(BF16)
            xb[:, HALF + c * LANES:HALF + (c + 1) * LANES] = hi.astype(BF16)
        gu = jnp.dot(xb[...], wgu_ref[...], preferred_element_type=F32) + bgu_ref[...]
        gate = jnp.minimum(gu[:, :D_EXPERT], SWIGLU_LIMIT)
        up = jnp.clip(gu[:, D_EXPERT:], -SWIGLU_LIMIT, SWIGLU_LIMIT)
        hid = (up + 1.0) * (gate * jax.nn.sigmoid(SWIGLU_ALPHA * gate))
        yb[...] = _pack_halves(jnp.dot(hid.astype(BF16), wd_ref[...], preferred_element_type=F32) + bd_ref[...])

    @pl.when(i == 0)
    def _():
        yb1[...] = jnp.zeros_like(yb1)
        for r in rows:
            gather(tok_ref, r, xs0, 0).start()

    @pl.when(jnp.logical_and(i < nt, i % 2 == 0))
    def _():
        step(0, xs0, xs1, yb0, yb1)

    @pl.when(jnp.logical_and(i < nt, i % 2 == 1))
    def _():
        step(1, xs1, xs0, yb1, yb0)

    def drain(s, xs, yb, yb_prev):
        for r in rows:
            gather(tok_ref, r, xs, s).wait()
        for r in rows:
            scatter(r, yb, s).wait()
        for r in rows:
            scatter(r, yb_prev, 1 - s).start()
        for r in rows:
            scatter(r, yb_prev, 1 - s).wait()
        for q, ybq in enumerate((yb0, yb1)):
            fill = pltpu.make_async_copy(ybq, y_hbm.at[pl.ds(spare0 + q * tm, tm), :], ssem.at[q])
            fill.start()
            fill.wait()

    @pl.when(jnp.logical_and(i == nt, i % 2 == 0))
    def _():
        drain(0, xs0, yb0, yb1)

    @pl.when(jnp.logical_and(i == nt, i % 2 == 1))
    def _():
        drain(1, xs1, yb1, yb0)


def _moe(xn, nt, te, tok, dstp, wgu, bgu, wd, bd, *, tm, y_rows, spare0):
    ntmax = te.shape[0]
    d = D_MODEL
    row_spec = lambda shift, nrows: pl.BlockSpec(
        (1, 1, tm), lambda i, nt_, te_: (jnp.minimum(i + shift, nrows - 1), 0, 0), memory_space=pltpu.SMEM)
    expert = lambda i, nt_, te_: (te_[jnp.minimum(i, ntmax - 1)], 0, 0)
    grid_spec = pltpu.PrefetchScalarGridSpec(
        num_scalar_prefetch=2,
        grid=(ntmax + 1,),
        in_specs=[
            row_spec(0, ntmax), row_spec(1, ntmax), row_spec(0, ntmax + 1),
            pl.BlockSpec(memory_space=pl.ANY),
            pl.BlockSpec((None, d, 2 * D_EXPERT), expert),
            pl.BlockSpec((None, 1, 2 * D_EXPERT), expert),
            pl.BlockSpec((None, D_EXPERT, d), expert),
            pl.BlockSpec((None, 1, d), expert),
        ],
        out_specs=pl.BlockSpec(memory_space=pl.ANY),
        scratch_shapes=[
            pltpu.VMEM((tm * SLAB_PITCH, LANES), U32),
            pltpu.VMEM((tm * SLAB_PITCH, LANES), U32),
            pltpu.VMEM((tm, HALF), U32),
            pltpu.VMEM((tm, HALF), U32),
            pltpu.VMEM((tm, d), BF16),
            pltpu.SemaphoreType.DMA((2,)),
            pltpu.SemaphoreType.DMA((2,)),
        ],
    )
    return pl.pallas_call(
        functools.partial(_moe_kernel, tm=tm, spare0=spare0),
        out_shape=jax.ShapeDtypeStruct((y_rows, HALF), U32),
        grid_spec=grid_spec,
        compiler_params=_params(("arbitrary",), 58),
        name="moe_experts",
    )(nt, te, tok, tok, dstp, xn, wgu, bgu, wd, bd)


INVERT_CHUNK = 8192


def _invert_kernel(slot_ref, table_ref, *, chunk, nslot):
    i = pl.program_id(0)

    @pl.when(i == 0)
    def _():
        def init(p, c):
            table_ref[p] = jnp.int32(-1)
            return c
        lax.fori_loop(0, nslot, init, 0, unroll=8)

    def body(a, c):
        table_ref[slot_ref[0, 0, a]] = i * chunk + a
        return c
    lax.fori_loop(0, chunk, body, 0, unroll=8)


def _invert(slot, nslot):
    n = slot.shape[0]
    chunk = min(INVERT_CHUNK, n)
    assert n % chunk == 0
    return pl.pallas_call(
        functools.partial(_invert_kernel, chunk=chunk, nslot=nslot),
        out_shape=jax.ShapeDtypeStruct((nslot,), jnp.int32),
        grid=(n // chunk,),
        in_specs=[pl.BlockSpec((1, 1, chunk), lambda i: (i, 0, 0), memory_space=pltpu.SMEM)],
        out_specs=pl.BlockSpec(memory_space=pltpu.SMEM),
        compiler_params=pltpu.CompilerParams(dimension_semantics=("arbitrary",)),
        name="invert_slots",
    )(slot.reshape(n // chunk, 1, chunk))


def _route(meta, cnt, t_all, tm):
    a_all = TOP_K * t_all
    ntmax = (a_all + N_EXPERTS * (tm - 1)) // tm
    cnt = cnt.reshape(N_EXPERTS).astype(jnp.int32)
    tiles_e = (cnt + tm - 1) // tm
    tile_end = jnp.cumsum(tiles_e)
    tile_start = tile_end - tiles_e
    nt = tile_end[-1:]
    ti = jnp.arange(ntmax, dtype=jnp.int32)
    te = jnp.minimum(jnp.sum((tile_end[None, :] <= ti[:, None]).astype(jnp.int32), axis=1), N_EXPERTS - 1)
    idx, rank = meta[:, :TOP_K], meta[:, TOP_K:2 * TOP_K]
    slot = tile_start[idx] * tm + rank
    table = _invert(slot.reshape(-1), ntmax * tm).reshape(ntmax, tm)
    valid = table >= 0
    tok = jnp.where(valid, table // TOP_K, 0)
    r = jnp.arange(tm, dtype=jnp.int32)[None, :]
    spare = a_all + (ti % 2)[:, None] * tm + r
    dst = jnp.where(valid, (table % TOP_K) * t_all + tok, spare)
    dstp = jnp.concatenate([a_all + tm + r, dst], axis=0)
    return nt.astype(jnp.int32), te, tok.reshape(ntmax, 1, tm), dstp.reshape(ntmax + 1, 1, tm)


def _combine_kernel(h_ref, y0_ref, y1_ref, y2_ref, y3_ref, gate_ref, g_ref, o_ref):
    acc_lo = h_ref[:, :HALF]
    acc_hi = h_ref[:, HALF:]
    gates = gate_ref[...]
    for k, y_ref in enumerate((y0_ref, y1_ref, y2_ref, y3_ref)):
        lo, hi = _unpack_halves(y_ref[...])
        acc_lo = acc_lo + gates[:, k:k + 1] * lo
        acc_hi = acc_hi + gates[:, k:k + 1] * hi
    ssq = jnp.sum(acc_lo * acc_lo, axis=-1, keepdims=True) + jnp.sum(acc_hi * acc_hi, axis=-1, keepdims=True)
    r = lax.rsqrt(ssq * (1.0 / D_MODEL) + EPS)
    o_ref[:, :HALF] = acc_lo * r * g_ref[:, :HALF]
    o_ref[:, HALF:] = acc_hi * r * g_ref[:, HALF:]


def _combine(h, y, gates, g, t_all, t0, *, tm):
    t, d = h.shape
    yspec = lambda k: pl.BlockSpec((tm, HALF), lambda i: ((k * t_all + t0) // tm + i, 0))
    return pl.pallas_call(
        _combine_kernel,
        out_shape=jax.ShapeDtypeStruct((t, d), F32),
        grid=(t // tm,),
        in_specs=[pl.BlockSpec((tm, d), lambda i: (i, 0)),
                  yspec(0), yspec(1), yspec(2), yspec(3),
                  pl.BlockSpec((tm, LANES), lambda i: (t0 // tm + i, 0)),
                  pl.BlockSpec((1, d), lambda i: (0, 0))],
        out_specs=pl.BlockSpec((tm, d), lambda i: (i, 0)),
        compiler_params=_params(("parallel",), 56),
        name="combine",
    )(h, y, y, y, y, gates, g.reshape(1, d))


def _mixer(x, mem, p, w_gate_up=None, w_down=None):
    b, s, d = x.shape
    t = b * s
    x2 = x.reshape(t, d)
    tabs = _rope_tables(s)
    wgu = wd = None
    if w_gate_up is not None:
        e, dd, f2 = w_gate_up.shape
        rb = _cast_rows(e * dd, f2, (t // _tile(s, PROJ_TM)) * (IN_WIDTH // PROJ_TN))
        if rb is None:
            wgu = w_gate_up.astype(BF16)
    if w_gate_up is None or wgu is not None:
        proj = _norm_proj(x2, p["g_mix"], p["w_in"], tabs, s)
    else:
        proj, wgu = _norm_proj(x2, p["g_mix"], p["w_in"], tabs, s, cast_src=w_gate_up.reshape(e * dd, f2), cast_rb=rb)
        wgu = wgu.reshape(e, dd, f2)
    proj3 = proj.reshape(b, s, IN_WIDTH)
    kv = _norm_proj(mem.reshape(b * N_MEM, d), p["g_mem"], p["w_mem_kv"], tm_pref=256)
    gg = p["g_group"]
    oa = _window_attn(proj3, p["attn_sink"], gg[:ATTN_WIDTH])
    osg = _sgu(proj3, p["w_spatial"], p["b_spatial"], p["g_sgu"], gg[ATTN_WIDTH:ATTN_WIDTH + SGU_WIDTH])
    ox = _mem_xattn(proj3, kv.reshape(b, N_MEM, 2 * XATTN_WIDTH), gg[ATTN_WIDTH + SGU_WIDTH:])
    mixed = (oa.reshape(t, -1), osg.reshape(t, -1), ox.reshape(t, -1))
    if w_down is None:
        return _out_proj(*mixed, p["w_out"], x2)
    e, f, dd = w_down.shape
    rb = _cast_rows(e * f, dd, (t // _tile(t, PROJ_TM)) * (D_MODEL // OUT_TN))
    if rb is None:
        return _out_proj(*mixed, p["w_out"], x2), wgu, w_down.astype(BF16)
    h, wd = _out_proj(*mixed, p["w_out"], x2, cast_src=w_down.reshape(e * f, dd), cast_rb=rb)
    return h, wgu, wd.reshape(e, f, dd)


def kernel(x_prompt, x_sample, mem_prompt, mem_sample, g_mix, w_in, attn_sink, g_sgu, w_spatial, b_spatial,
           g_mem, w_mem_kv, g_group, w_out, g_ffn, w_router, b_router, w_gate_up, b_gate_up, w_down, b_down,
           g_final):
    assert g_mix.shape[0] == 1
    q, k, v, u, vs, xq = jnp.split(w_in[0], [2048, 2304, 2560, 3584, 4608], axis=1)
    p = {
        "g_mix": g_mix[0],
        "w_in": jnp.concatenate([q, u, vs, xq, k, v], axis=1).astype(BF16),
        "attn_sink": attn_sink[0],
        "g_sgu": g_sgu[0],
        "w_spatial": w_spatial[0].astype(BF16),
        "b_spatial": jnp.repeat(b_spatial[0].T, SGU_GROUP_DIM, axis=1),
        "g_mem": g_mem[0],
        "w_mem_kv": w_mem_kv[0].astype(BF16),
        "g_group": g_group[0],
        "w_out": w_out[0].astype(BF16),
    }
    h1 = _mixer(x_prompt, mem_prompt, p)
    h2, wgu, wd = _mixer(x_sample, mem_sample, p, w_gate_up[0], w_down[0])
    t1, t2 = h1.shape[0], h2.shape[0]
    t_all = t1 + t2
    tm = 256
    assert t1 % tm == 0 and t2 % tm == 0
    xn, meta, gates, cnt = _router(h1, h2, g_ffn[0], w_router[0].astype(BF16), b_router[0], tm=tm)
    nt, te, tok, dstp = _route(meta, cnt, t_all, tm)
    spare0 = TOP_K * t_all
    y = _moe(xn, nt, te, tok, dstp, wgu, b_gate_up[0][:, None, :], wd, b_down[0][:, None, :],
             tm=tm, y_rows=spare0 + 2 * tm, spare0=spare0)
    out1 = _combine(h1, y, gates, g_final, t_all, 0, tm=tm).reshape(x_prompt.shape)
    out2 = _combine(h2, y, gates, g_final, t_all, t1, tm=tm).reshape(x_sample.shape)
    return out1, out2
```

```python
import functools

import jax
import jax.numpy as jnp
import numpy as np
from jax import lax
from jax.experimental import pallas as pl
from jax.experimental.pallas import tpu as pltpu

D_MODEL = 4096
N_MEM = 256
HEAD_DIM = 64
ATTN_WIDTH = 2048
ATTN_HEADS = 32
KV_HEADS = 4
KV_WIDTH = 256
WINDOW = 128
BLOCK = 128
ROPE_DIM = 16
ROPE_THETA = 500000.0
SGU_WIDTH = 1024
SGU_GROUPS = 4
SGU_GROUP_DIM = 256
CHUNK = 128
XATTN_WIDTH = 1024
XATTN_HEADS = 4
XATTN_HEAD_DIM = 256
IN_WIDTH = 5632
N_EXPERTS = 32
TOP_K = 4
D_EXPERT = 512
SWIGLU_LIMIT = 7.0
SWIGLU_ALPHA = 1.702
EPS = 1e-5
NEG_INF = -1e30

LANES = 128
COL_Q, COL_U, COL_VS, COL_XQ, COL_K, COL_V = 0, 2048, 3072, 4096, 5120, 5376

F32 = jnp.float32
BF16 = jnp.bfloat16
MIB = 1024 * 1024


def _params(semantics, vmem_mib):
    return pltpu.CompilerParams(dimension_semantics=semantics, vmem_limit_bytes=vmem_mib * MIB)


def _tile(n, pref):
    t = min(n, pref)
    while n % t or t % LANES:
        t -= LANES
    assert t > 0
    return t


def _rope_group(a, c, s1, s2):
    return a * c + pltpu.roll(a, LANES - ROPE_DIM // 2, 1) * s1 + pltpu.roll(a, ROPE_DIM // 2, 1) * s2


PROJ_TM, PROJ_TN, OUT_TN = 512, 512, 1024
CAST_BLOCK_BYTES = 2 * MIB


def _cast_rows(rows, cols, steps):
    nblk = 1 << (steps.bit_length() - 1)
    while rows % nblk:
        nblk //= 2
    rb = rows // nblk
    return rb if rb * cols * 4 <= CAST_BLOCK_BYTES and rb % 16 == 0 else None


def _cast_spec(nblk, rb, cols, steps_per_row):
    return pl.BlockSpec((rb, cols), lambda i, j: (jnp.minimum(i * steps_per_row + j, nblk - 1), 0))


def _norm_proj_kernel(*refs, rope, tn, cast):
    if cast:
        *ins, cast_in_ref, o_ref, cast_out_ref, xn_ref = refs
        refs = (*ins, o_ref, xn_ref)
        cast_out_ref[...] = cast_in_ref[...].astype(BF16)
    if rope:
        x_ref, g_ref, w_ref, c_ref, s1_ref, s2_ref, o_ref, xn_ref = refs
    else:
        x_ref, g_ref, w_ref, o_ref, xn_ref = refs
    j = pl.program_id(1)

    @pl.when(j == 0)
    def _():
        x = x_ref[...]
        ms = jnp.mean(x * x, axis=-1, keepdims=True)
        xn_ref[...] = (x * lax.rsqrt(ms + EPS) * g_ref[...]).astype(BF16)

    acc = jnp.dot(xn_ref[...], w_ref[...], preferred_element_type=F32)
    if not rope:
        o_ref[...] = acc.astype(o_ref.dtype)
        return

    ngroups = tn // LANES
    q_tiles = ATTN_WIDTH // tn
    k_tile = COL_K // tn
    k_groups = KV_WIDTH // LANES

    def store(n_rope):
        c, s1, s2 = c_ref[...], s1_ref[...], s2_ref[...]
        for gidx in range(ngroups):
            a = acc[:, gidx * LANES:(gidx + 1) * LANES]
            if gidx < n_rope:
                a = _rope_group(a, c, s1, s2)
            o_ref[:, gidx * LANES:(gidx + 1) * LANES] = a.astype(o_ref.dtype)

    @pl.when(j < q_tiles)
    def _():
        store(ngroups)

    @pl.when(j == k_tile)
    def _():
        store(k_groups)

    @pl.when(jnp.logical_and(j >= q_tiles, j != k_tile))
    def _():
        store(0)


def _norm_proj(x, g, w, rope_tabs=None, seq=None, *, tm_pref=PROJ_TM, tn=PROJ_TN, cast_src=None, cast_rb=None):
    t, d = x.shape
    n = w.shape[1]
    tm = _tile(t if seq is None else seq, tm_pref)
    rope = rope_tabs is not None
    cast = cast_src is not None
    in_specs = [
        pl.BlockSpec((tm, d), lambda i, j: (i, 0)),
        pl.BlockSpec((1, d), lambda i, j: (0, 0)),
        pl.BlockSpec((d, tn), lambda i, j: (0, j)),
    ]
    args = [x, g.reshape(1, d), w]
    if rope:
        assert COL_K % tn == 0 and ATTN_WIDTH % tn == 0 and tn >= KV_WIDTH
        sblocks = seq // tm
        tab_spec = pl.BlockSpec((tm, LANES), lambda i, j: (i % sblocks, 0))
        in_specs += [tab_spec, tab_spec, tab_spec]
        args += list(rope_tabs)
    out_shape = jax.ShapeDtypeStruct((t, n), BF16)
    out_specs = pl.BlockSpec((tm, tn), lambda i, j: (i, j))
    if cast:
        rows, cols = cast_src.shape
        nblk = rows // cast_rb
        assert rows % cast_rb == 0 and nblk <= (t // tm) * (n // tn)
        spec = _cast_spec(nblk, cast_rb, cols, n // tn)
        in_specs.append(spec)
        args.append(cast_src)
        out_shape = (out_shape, jax.ShapeDtypeStruct((rows, cols), BF16))
        out_specs = (out_specs, spec)
    return pl.pallas_call(
        functools.partial(_norm_proj_kernel, rope=rope, tn=tn, cast=cast),
        out_shape=out_shape,
        grid=(t // tm, n // tn),
        in_specs=in_specs,
        out_specs=out_specs,
        scratch_shapes=[pltpu.VMEM((tm, d), BF16)],
        compiler_params=_params(("arbitrary", "arbitrary") if cast else ("parallel", "arbitrary"), 48),
        name="norm_proj_rope" if rope else "norm_proj",
    )(*args)


def _rope_tables(seq):
    half = ROPE_DIM // 2
    inv_freq = ROPE_THETA ** (-jnp.arange(0, ROPE_DIM, 2, dtype=F32) / ROPE_DIM)
    ang = jnp.arange(seq, dtype=F32)[:, None] * inv_freq[None, :]
    cos, sin = jnp.cos(ang), jnp.sin(ang)
    dim = np.arange(LANES) % HEAD_DIM
    sel = dim % half
    cos_l, sin_l = cos[:, sel], sin[:, sel]
    c = jnp.where(dim < ROPE_DIM, cos_l, 1.0)
    s1 = jnp.where(dim < half, -sin_l, 0.0)
    s2 = jnp.where((dim >= half) & (dim < ROPE_DIM), sin_l, 0.0)
    return c.astype(F32), s1.astype(F32), s2.astype(F32)


def _window_attn_kernel(sink_ref, q_ref, kp_ref, kc_ref, kn_ref, vp_ref, vc_ref, vn_ref, gg_ref,
                        o_ref, obuf_ref, *, nb):
    n = pl.program_id(1)
    band = 3 * BLOCK
    qi = lax.broadcasted_iota(jnp.int32, (BLOCK, band), 0)
    ki = lax.broadcasted_iota(jnp.int32, (BLOCK, band), 1)
    rel = ki - BLOCK - qi
    lo = jnp.where(n == 0, BLOCK, 0)
    hi = jnp.where(n == nb - 1, 2 * BLOCK, band)
    valid = (jnp.abs(rel) <= WINDOW) & (ki >= lo) & (ki < hi)
    lane_k = lax.broadcasted_iota(jnp.int32, (band, LANES), 1)
    left_k = lane_k < HEAD_DIM
    lane_q = lax.broadcasted_iota(jnp.int32, (BLOCK, LANES), 1)
    left_q = lane_q < HEAD_DIM
    ones_l = jnp.where(left_k, 1.0, 0.0).astype(F32)
    ones_r = 1.0 - ones_l
    scale = HEAD_DIM ** -0.5
    ssq = jnp.zeros((BLOCK, 1), F32)

    for h in range(KV_HEADS):
        slab = (h // 2) * LANES
        k3 = jnp.concatenate([r[:, slab:slab + LANES] for r in (kp_ref, kc_ref, kn_ref)], axis=0).astype(F32)
        v3 = jnp.concatenate([r[:, slab:slab + LANES] for r in (vp_ref, vc_ref, vn_ref)], axis=0).astype(F32)
        k3r = pltpu.roll(k3, HEAD_DIM, 1)
        v3r = pltpu.roll(v3, HEAD_DIM, 1)
        if h % 2 == 0:
            ka, kb, va, vb = k3, k3r, v3, v3r
        else:
            ka, kb, va, vb = k3r, k3, v3r, v3
        kbd = jnp.concatenate([jnp.where(left_k, ka, 0.0), jnp.where(left_k, 0.0, kb)], axis=0).astype(BF16)
        vbd = jnp.concatenate(
            [jnp.concatenate([jnp.where(left_k, va, 0.0), ones_l], axis=1),
             jnp.concatenate([jnp.where(left_k, 0.0, vb), ones_r], axis=1)], axis=0).astype(BF16)
        for p in range(ATTN_HEADS // KV_HEADS // 2):
            g = h * 4 + p
            q2 = q_ref[:, g * LANES:(g + 1) * LANES]
            s2 = lax.dot_general(q2, kbd, (((1,), (1,)), ((), ())), preferred_element_type=F32)
            sa = jnp.where(valid, s2[:, :band] * scale, NEG_INF)
            sb = jnp.where(valid, s2[:, band:] * scale, NEG_INF)
            sink_a = sink_ref[2 * g]
            sink_b = sink_ref[2 * g + 1]
            ma = jnp.maximum(jnp.max(sa, axis=-1, keepdims=True), sink_a)
            mb = jnp.maximum(jnp.max(sb, axis=-1, keepdims=True), sink_b)
            p2 = jnp.concatenate([jnp.exp(sa - ma), jnp.exp(sb - mb)], axis=1).astype(BF16)
            o2 = jnp.dot(p2, vbd, preferred_element_type=F32)
            den = o2[:, LANES:] + jnp.where(left_q, jnp.exp(sink_a - ma), jnp.exp(sink_b - mb))
            out = o2[:, :LANES] / den
            ssq = ssq + jnp.sum(out * out, axis=-1, keepdims=True)
            obuf_ref[:, g * LANES:(g + 1) * LANES] = out

    r = lax.rsqrt(ssq * (1.0 / ATTN_WIDTH) + EPS)
    o_ref[...] = (obuf_ref[...] * r * gg_ref[...]).astype(o_ref.dtype)


def _window_attn(proj3, sink, gg):
    b, s, _ = proj3.shape
    nb = s // BLOCK
    kcol, vcol = COL_K // KV_WIDTH, COL_V // KV_WIDTH

    def band_spec(col, shift):
        return pl.BlockSpec((None, BLOCK, KV_WIDTH),
                            lambda bi, n: (bi, jnp.clip(n + shift, 0, nb - 1), col))

    return pl.pallas_call(
        functools.partial(_window_attn_kernel, nb=nb),
        out_shape=jax.ShapeDtypeStruct((b, s, ATTN_WIDTH), BF16),
        grid=(b, nb),
        in_specs=[
            pl.BlockSpec(memory_space=pltpu.SMEM),
            pl.BlockSpec((None, BLOCK, ATTN_WIDTH), lambda bi, n: (bi, n, 0)),
            band_spec(kcol, -1), band_spec(kcol, 0), band_spec(kcol, 1),
            band_spec(vcol, -1), band_spec(vcol, 0), band_spec(vcol, 1),
            pl.BlockSpec((1, ATTN_WIDTH), lambda bi, n: (0, 0)),
        ],
        out_specs=pl.BlockSpec((None, BLOCK, ATTN_WIDTH), lambda bi, n: (bi, n, 0)),
        scratch_shapes=[pltpu.VMEM((BLOCK, ATTN_WIDTH), F32)],
        compiler_params=_params(("parallel", "parallel"), 32),
        name="window_attn",
    )(sink, proj3, proj3, proj3, proj3, proj3, proj3, proj3, gg.reshape(1, ATTN_WIDTH))


def _sgu_kernel(u_ref, v_ref, ws_ref, bs_ref, gs_ref, gg_ref, o_ref, *, chunks):
    for c in range(chunks):
        rows = slice(c * CHUNK, (c + 1) * CHUNK)
        ug = jax.nn.gelu(u_ref[rows, :].astype(F32))
        vg = jax.nn.gelu(v_ref[rows, :].astype(F32))
        mixed = []
        for h in range(SGU_GROUPS):
            cols = slice(h * SGU_GROUP_DIM, (h + 1) * SGU_GROUP_DIM)
            vh = vg[:, cols]
            r = lax.rsqrt(jnp.mean(vh * vh, axis=-1, keepdims=True) + EPS)
            vn = (vh * r * gs_ref[:, cols]).astype(BF16)
            mixed.append(jnp.dot(ws_ref[h], vn, preferred_element_type=F32))
        o = ug * (jnp.concatenate(mixed, axis=1) + bs_ref[...])
        r = lax.rsqrt(jnp.mean(o * o, axis=-1, keepdims=True) + EPS)
        o_ref[rows, :] = (o * r * gg_ref[...]).astype(o_ref.dtype)


def _sgu(proj3, ws, bs_full, gs, gg, *, rows_pref=512):
    b, s, _ = proj3.shape
    tr = _tile(s, rows_pref)
    const = lambda bi, n: (0, 0)
    return pl.pallas_call(
        functools.partial(_sgu_kernel, chunks=tr // CHUNK),
        out_shape=jax.ShapeDtypeStruct((b, s, SGU_WIDTH), BF16),
        grid=(b, s // tr),
        in_specs=[
            pl.BlockSpec((None, tr, SGU_WIDTH), lambda bi, n: (bi, n, COL_U // SGU_WIDTH)),
            pl.BlockSpec((None, tr, SGU_WIDTH), lambda bi, n: (bi, n, COL_VS // SGU_WIDTH)),
            pl.BlockSpec((SGU_GROUPS, CHUNK, CHUNK), lambda bi, n: (0, 0, 0)),
            pl.BlockSpec((CHUNK, SGU_WIDTH), const),
            pl.BlockSpec((1, SGU_WIDTH), const),
            pl.BlockSpec((1, SGU_WIDTH), const),
        ],
        out_specs=pl.BlockSpec((None, tr, SGU_WIDTH), lambda bi, n: (bi, n, 0)),
        compiler_params=_params(("parallel", "parallel"), 32),
        name="sgu",
    )(proj3, proj3, ws, bs_full, gs.reshape(1, SGU_WIDTH), gg.reshape(1, SGU_WIDTH))


def _mem_xattn_kernel(q_ref, kv_ref, gg_ref, o_ref):
    scale = XATTN_HEAD_DIM ** -0.5
    outs = []
    ssq = None
    for h in range(XATTN_HEADS):
        cols = slice(h * XATTN_HEAD_DIM, (h + 1) * XATTN_HEAD_DIM)
        vcols = slice(XATTN_WIDTH + h * XATTN_HEAD_DIM, XATTN_WIDTH + (h + 1) * XATTN_HEAD_DIM)
        s = lax.dot_general(q_ref[:, cols], kv_ref[:, cols], (((1,), (1,)), ((), ())),
                            preferred_element_type=F32) * scale
        m = jnp.max(s, axis=-1, keepdims=True)
        p = jnp.exp(s - m)
        l = jnp.sum(p, axis=-1, keepdims=True)
        o = jnp.dot(p.astype(BF16), kv_ref[:, vcols], preferred_element_type=F32) / l
        sq = jnp.sum(o * o, axis=-1, keepdims=True)
        ssq = sq if ssq is None else ssq + sq
        outs.append(o)
    r = lax.rsqrt(ssq * (1.0 / XATTN_WIDTH) + EPS)
    for h in range(XATTN_HEADS):
        cols = slice(h * XATTN_HEAD_DIM, (h + 1) * XATTN_HEAD_DIM)
        o_ref[:, cols] = (outs[h] * r * gg_ref[:, cols]).astype(o_ref.dtype)


def _mem_xattn(proj3, kv3, gg, *, rows_pref=512):
    b, s, _ = proj3.shape
    tq = _tile(s, rows_pref)
    return pl.pallas_call(
        _mem_xattn_kernel,
        out_shape=jax.ShapeDtypeStruct((b, s, XATTN_WIDTH), BF16),
        grid=(b, s // tq),
        in_specs=[
            pl.BlockSpec((None, tq, XATTN_WIDTH), lambda bi, n: (bi, n, COL_XQ // XATTN_WIDTH)),
            pl.BlockSpec((None, N_MEM, 2 * XATTN_WIDTH), lambda bi, n: (bi, 0, 0)),
            pl.BlockSpec((1, XATTN_WIDTH), lambda bi, n: (0, 0)),
        ],
        out_specs=pl.BlockSpec((None, tq, XATTN_WIDTH), lambda bi, n: (bi, n, 0)),
        compiler_params=_params(("parallel", "parallel"), 32),
        name="mem_xattn",
    )(proj3, kv3, gg.reshape(1, XATTN_WIDTH))


def _out_proj_kernel(oa_ref, os_ref, ox_ref, w_ref, x_ref, *rest):
    if len(rest) == 3:
        cast_in_ref, h_ref, cast_out_ref = rest
        cast_out_ref[...] = cast_in_ref[...].astype(BF16)
    else:
        (h_ref,) = rest
    a1, a2 = ATTN_WIDTH, ATTN_WIDTH + SGU_WIDTH
    acc = jnp.dot(oa_ref[...], w_ref[:a1, :], preferred_element_type=F32)
    acc += jnp.dot(os_ref[...], w_ref[a1:a2, :], preferred_element_type=F32)
    acc += jnp.dot(ox_ref[...], w_ref[a2:, :], preferred_element_type=F32)
    h_ref[...] = x_ref[...] + acc


def _out_proj(oa, osg, ox, w, x, *, tm_pref=PROJ_TM, tn=OUT_TN, cast_src=None, cast_rb=None):
    t, d = x.shape
    tm = _tile(t, tm_pref)
    cast = cast_src is not None
    in_specs = [
        pl.BlockSpec((tm, ATTN_WIDTH), lambda i, j: (i, 0)),
        pl.BlockSpec((tm, SGU_WIDTH), lambda i, j: (i, 0)),
        pl.BlockSpec((tm, XATTN_WIDTH), lambda i, j: (i, 0)),
        pl.BlockSpec((d, tn), lambda i, j: (0, j)),
        pl.BlockSpec((tm, tn), lambda i, j: (i, j)),
    ]
    args = [oa, osg, ox, w, x]
    out_shape = jax.ShapeDtypeStruct((t, d), F32)
    out_specs = pl.BlockSpec((tm, tn), lambda i, j: (i, j))
    if cast:
        rows, cols = cast_src.shape
        nblk = rows // cast_rb
        assert rows % cast_rb == 0 and nblk <= (t // tm) * (d // tn)
        spec = _cast_spec(nblk, cast_rb, cols, d // tn)
        in_specs.append(spec)
        args.append(cast_src)
        out_shape = (out_shape, jax.ShapeDtypeStruct((rows, cols), BF16))
        out_specs = (out_specs, spec)
    return pl.pallas_call(
        _out_proj_kernel,
        out_shape=out_shape,
        grid=(t // tm, d // tn),
        in_specs=in_specs,
        out_specs=out_specs,
        compiler_params=_params(("arbitrary", "arbitrary") if cast else ("parallel", "arbitrary"), 48),
        name="out_proj",
    )(*args)


SLAB_ROWS = D_MODEL // LANES
SLAB_PITCH = SLAB_ROWS + 4


def _router_kernel(h1_ref, h2_ref, g_ref, wr_ref, br_ref, xn_ref, meta_ref, gate_ref, cnt_ref, run_ref,
                   *, nb1, tm):
    i = pl.program_id(0)

    @pl.when(i == 0)
    def _():
        run_ref[...] = jnp.zeros_like(run_ref)

    def body(h_ref):
        x = h_ref[...]
        ms = jnp.mean(x * x, axis=-1, keepdims=True)
        xn = x * lax.rsqrt(ms + EPS) * g_ref[...]
        for c in range(SLAB_ROWS):
            xn_ref[pl.ds(c, tm, stride=SLAB_PITCH), :] = xn[:, c * LANES:(c + 1) * LANES]
        for c in range(SLAB_ROWS, SLAB_PITCH):
            xn_ref[pl.ds(c, tm, stride=SLAB_PITCH), :] = jnp.zeros((tm, LANES), F32)
        logits = jnp.dot(xn.astype(BF16), wr_ref[...], preferred_element_type=F32) + br_ref[...]
        lane_e = lax.broadcasted_iota(jnp.int32, (tm, N_EXPERTS), 1)
        lane_o = lax.broadcasted_iota(jnp.int32, (tm, LANES), 1)
        row = lax.broadcasted_iota(jnp.int32, (tm, tm), 0)
        col = lax.broadcasted_iota(jnp.int32, (tm, tm), 1)
        earlier = jnp.where(col < row, 1.0, 0.0).astype(BF16)
        vals, idxs = [], []
        l = logits
        for _ in range(TOP_K):
            m = jnp.max(l, axis=-1, keepdims=True)
            e = jnp.min(jnp.where(l == m, lane_e, N_EXPERTS), axis=-1, keepdims=True)
            vals.append(m)
            idxs.append(e)
            l = jnp.where(lane_e == e, -jnp.inf, l)
        es = [jnp.exp(v - vals[0]) for v in vals]
        den = es[0] + es[1] + es[2] + es[3]
        base = run_ref[...]
        meta = jnp.zeros((tm, LANES), jnp.int32)
        gate_out = jnp.zeros((tm, LANES), F32)
        for k in range(TOP_K):
            onehot = jnp.where(lane_e == idxs[k], 1.0, 0.0)
            before = jnp.dot(earlier, onehot.astype(BF16), preferred_element_type=F32)
            rank = jnp.sum(onehot * (before + base), axis=-1, keepdims=True)
            base = base + jnp.sum(onehot, axis=0, keepdims=True)
            meta = jnp.where(lane_o == k, idxs[k], meta)
            meta = jnp.where(lane_o == TOP_K + k, rank.astype(jnp.int32), meta)
            gate_out = jnp.where(lane_o == k, es[k] / den, gate_out)
        run_ref[...] = base
        cnt_ref[...] = base
        meta_ref[...] = meta
        gate_ref[...] = gate_out

    @pl.when(i < nb1)
    def _():
        body(h1_ref)

    @pl.when(i >= nb1)
    def _():
        body(h2_ref)


def _router(h1, h2, g, wr, br, *, tm):
    t1, d = h1.shape
    t2 = h2.shape[0]
    nb1, nb2 = t1 // tm, t2 // tm
    t_all = t1 + t2
    const = lambda i: (0, 0)
    return pl.pallas_call(
        functools.partial(_router_kernel, nb1=nb1, tm=tm),
        out_shape=(jax.ShapeDtypeStruct((t_all * SLAB_PITCH, LANES), F32),
                   jax.ShapeDtypeStruct((t_all, LANES), jnp.int32),
                   jax.ShapeDtypeStruct((t_all, LANES), F32),
                   jax.ShapeDtypeStruct((1, N_EXPERTS), F32)),
        grid=(nb1 + nb2,),
        in_specs=[
            pl.BlockSpec((tm, d), lambda i: (jnp.minimum(i, nb1 - 1), 0)),
            pl.BlockSpec((tm, d), lambda i: (jnp.maximum(i - nb1, 0), 0)),
            pl.BlockSpec((1, d), const),
            pl.BlockSpec((d, N_EXPERTS), const),
            pl.BlockSpec((1, N_EXPERTS), const),
        ],
        out_specs=(pl.BlockSpec((tm * SLAB_PITCH, LANES), lambda i: (i, 0)),
                   pl.BlockSpec((tm, LANES), lambda i: (i, 0)),
                   pl.BlockSpec((tm, LANES), lambda i: (i, 0)),
                   pl.BlockSpec((1, N_EXPERTS), const)),
        scratch_shapes=[pltpu.VMEM((1, N_EXPERTS), F32)],
        compiler_params=_params(("arbitrary",), 48),
        name="router",
    )(h1, h2, g.reshape(1, d), wr, br.reshape(1, N_EXPERTS))


def _moe_kernel(nt_ref, te_ref, tok_ref, tokn_ref, dstp_ref, xn_hbm, wgu_ref, bgu_ref, wd_ref, bd_ref,
                y_hbm, xs0, xs1, yb0, yb1, xb, gsem, ssem, *, tm, spare0):
    del te_ref
    i = pl.program_id(0)
    nt = nt_ref[0]
    rows = range(tm)

    def gather(idx_ref, r, xs, s):
        return pltpu.make_async_copy(xn_hbm.at[pl.ds(idx_ref[0, 0, r] * SLAB_PITCH, SLAB_ROWS), :],
                                     xs.at[pl.ds(r * SLAB_PITCH, SLAB_ROWS), :], gsem.at[s])

    def scatter(r, yb, s):
        return pltpu.make_async_copy(yb.at[pl.ds(r, 1), :], y_hbm.at[pl.ds(dstp_ref[0, 0, r], 1), :], ssem.at[s])

    def step(s, xs, xs_next, yb, yb_prev):
        for r in rows:
            gather(tok_ref, r, xs, s).wait()

        @pl.when(i >= 1)
        def _():
            for r in rows:
                scatter(r, yb, s).wait()

        @pl.when(nt > 0)
        def _():
            for r in rows:
                gather(tokn_ref, r, xs_next, 1 - s).start()
                scatter(r, yb_prev, 1 - s).start()
        for c in range(SLAB_ROWS):
            xb[:, c * LANES:(c + 1) * LANES] = xs[pl.ds(c, tm, stride=SLAB_PITCH), :].astype(BF16)
        gu = jnp.dot(xb[...], wgu_ref[...], preferred_element_type=F32) + bgu_ref[...]
        gate = jnp.minimum(gu[:, :D_EXPERT], SWIGLU_LIMIT)
        up = jnp.clip(gu[:, D_EXPERT:], -SWIGLU_LIMIT, SWIGLU_LIMIT)
        hid = (up + 1.0) * (gate * jax.nn.sigmoid(SWIGLU_ALPHA * gate))
        yb[...] = jnp.dot(hid.astype(BF16), wd_ref[...], preferred_element_type=F32) + bd_ref[...]

    @pl.when(i == 0)
    def _():
        yb1[...] = jnp.zeros_like(yb1)
        for r in rows:
            gather(tok_ref, r, xs0, 0).start()

    @pl.when(jnp.logical_and(i < nt, i % 2 == 0))
    def _():
        step(0, xs0, xs1, yb0, yb1)

    @pl.when(jnp.logical_and(i < nt, i % 2 == 1))
    def _():
        step(1, xs1, xs0, yb1, yb0)

    def drain(s, xs, yb, yb_prev):
        for r in rows:
            gather(tok_ref, r, xs, s).wait()
        for r in rows:
            scatter(r, yb, s).wait()
        for r in rows:
            scatter(r, yb_prev, 1 - s).start()
        for r in rows:
            scatter(r, yb_prev, 1 - s).wait()
        for q, ybq in enumerate((yb0, yb1)):
            fill = pltpu.make_async_copy(ybq, y_hbm.at[pl.ds(spare0 + q * tm, tm), :], ssem.at[q])
            fill.start()
            fill.wait()

    @pl.when(jnp.logical_and(i == nt, i % 2 == 0))
    def _():
        drain(0, xs0, yb0, yb1)

    @pl.when(jnp.logical_and(i == nt, i % 2 == 1))
    def _():
        drain(1, xs1, yb1, yb0)


def _moe(xn, nt, te, tok, dstp, wgu, bgu, wd, bd, *, tm, y_rows, spare0):
    ntmax = te.shape[0]
    d = D_MODEL
    row_spec = lambda shift, nrows: pl.BlockSpec(
        (1, 1, tm), lambda i, nt_, te_: (jnp.minimum(i + shift, nrows - 1), 0, 0), memory_space=pltpu.SMEM)
    expert = lambda i, nt_, te_: (te_[jnp.minimum(i, ntmax - 1)], 0, 0)
    grid_spec = pltpu.PrefetchScalarGridSpec(
        num_scalar_prefetch=2,
        grid=(ntmax + 1,),
        in_specs=[
            row_spec(0, ntmax), row_spec(1, ntmax), row_spec(0, ntmax + 1),
            pl.BlockSpec(memory_space=pl.ANY),
            pl.BlockSpec((None, d, 2 * D_EXPERT), expert),
            pl.BlockSpec((None, 1, 2 * D_EXPERT), expert),
            pl.BlockSpec((None, D_EXPERT, d), expert),
            pl.BlockSpec((None, 1, d), expert),
        ],
        out_specs=pl.BlockSpec(memory_space=pl.ANY),
        scratch_shapes=[
            pltpu.VMEM((tm * SLAB_PITCH, LANES), F32),
            pltpu.VMEM((tm * SLAB_PITCH, LANES), F32),
            pltpu.VMEM((tm, d), F32),
            pltpu.VMEM((tm, d), F32),
            pltpu.VMEM((tm, d), BF16),
            pltpu.SemaphoreType.DMA((2,)),
            pltpu.SemaphoreType.DMA((2,)),
        ],
    )
    return pl.pallas_call(
        functools.partial(_moe_kernel, tm=tm, spare0=spare0),
        out_shape=jax.ShapeDtypeStruct((y_rows, d), F32),
        grid_spec=grid_spec,
        compiler_params=_params(("arbitrary",), 58),
        name="moe_experts",
    )(nt, te, tok, tok, dstp, xn, wgu, bgu, wd, bd)


INVERT_CHUNK = 8192


def _invert_kernel(slot_ref, table_ref, *, chunk, nslot):
    i = pl.program_id(0)

    @pl.when(i == 0)
    def _():
        def init(p, c):
            table_ref[p] = jnp.int32(-1)
            return c
        lax.fori_loop(0, nslot, init, 0, unroll=8)

    def body(a, c):
        table_ref[slot_ref[0, 0, a]] = i * chunk + a
        return c
    lax.fori_loop(0, chunk, body, 0, unroll=8)


def _invert(slot, nslot):
    n = slot.shape[0]
    chunk = min(INVERT_CHUNK, n)
    assert n % chunk == 0
    return pl.pallas_call(
        functools.partial(_invert_kernel, chunk=chunk, nslot=nslot),
        out_shape=jax.ShapeDtypeStruct((nslot,), jnp.int32),
        grid=(n // chunk,),
        in_specs=[pl.BlockSpec((1, 1, chunk), lambda i: (i, 0, 0), memory_space=pltpu.SMEM)],
        out_specs=pl.BlockSpec(memory_space=pltpu.SMEM),
        compiler_params=pltpu.CompilerParams(dimension_semantics=("arbitrary",)),
        name="invert_slots",
    )(slot.reshape(n // chunk, 1, chunk))


def _route(meta, cnt, t_all, tm):
    a_all = TOP_K * t_all
    ntmax = (a_all + N_EXPERTS * (tm - 1)) // tm
    cnt = cnt.reshape(N_EXPERTS).astype(jnp.int32)
    tiles_e = (cnt + tm - 1) // tm
    tile_end = jnp.cumsum(tiles_e)
    tile_start = tile_end - tiles_e
    nt = tile_end[-1:]
    ti = jnp.arange(ntmax, dtype=jnp.int32)
    te = jnp.minimum(jnp.sum((tile_end[None, :] <= ti[:, None]).astype(jnp.int32), axis=1), N_EXPERTS - 1)
    idx, rank = meta[:, :TOP_K], meta[:, TOP_K:2 * TOP_K]
    slot = tile_start[idx] * tm + rank
    table = _invert(slot.reshape(-1), ntmax * tm).reshape(ntmax, tm)
    valid = table >= 0
    tok = jnp.where(valid, table // TOP_K, 0)
    r = jnp.arange(tm, dtype=jnp.int32)[None, :]
    spare = a_all + (ti % 2)[:, None] * tm + r
    dst = jnp.where(valid, (table % TOP_K) * t_all + tok, spare)
    dstp = jnp.concatenate([a_all + tm + r, dst], axis=0)
    return nt.astype(jnp.int32), te, tok.reshape(ntmax, 1, tm), dstp.reshape(ntmax + 1, 1, tm)


def _combine_kernel(h_ref, y0_ref, y1_ref, y2_ref, y3_ref, gate_ref, g_ref, o_ref):
    acc = h_ref[...]
    gates = gate_ref[...]
    for k, y_ref in enumerate((y0_ref, y1_ref, y2_ref, y3_ref)):
        acc = acc + gates[:, k:k + 1] * y_ref[...]
    ms = jnp.mean(acc * acc, axis=-1, keepdims=True)
    o_ref[...] = acc * lax.rsqrt(ms + EPS) * g_ref[...]


def _combine(h, y, gates, g, t_all, t0, *, tm):
    t, d = h.shape
    yspec = lambda k: pl.BlockSpec((tm, d), lambda i: ((k * t_all + t0) // tm + i, 0))
    return pl.pallas_call(
        _combine_kernel,
        out_shape=jax.ShapeDtypeStruct((t, d), F32),
        grid=(t // tm,),
        in_specs=[pl.BlockSpec((tm, d), lambda i: (i, 0)),
                  yspec(0), yspec(1), yspec(2), yspec(3),
                  pl.BlockSpec((tm, LANES), lambda i: (t0 // tm + i, 0)),
                  pl.BlockSpec((1, d), lambda i: (0, 0))],
        out_specs=pl.BlockSpec((tm, d), lambda i: (i, 0)),
        compiler_params=_params(("parallel",), 56),
        name="combine",
    )(h, y, y, y, y, gates, g.reshape(1, d))


def _mixer(x, mem, p, w_gate_up=None, w_down=None):
    b, s, d = x.shape
    t = b * s
    x2 = x.reshape(t, d)
    tabs = _rope_tables(s)
    wgu = wd = None
    if w_gate_up is not None:
        e, dd, f2 = w_gate_up.shape
        rb = _cast_rows(e * dd, f2, (t // _tile(s, PROJ_TM)) * (IN_WIDTH // PROJ_TN))
        if rb is None:
            wgu = w_gate_up.astype(BF16)
    if w_gate_up is None or wgu is not None:
        proj = _norm_proj(x2, p["g_mix"], p["w_in"], tabs, s)
    else:
        proj, wgu = _norm_proj(x2, p["g_mix"], p["w_in"], tabs, s, cast_src=w_gate_up.reshape(e * dd, f2), cast_rb=rb)
        wgu = wgu.reshape(e, dd, f2)
    proj3 = proj.reshape(b, s, IN_WIDTH)
    kv = _norm_proj(mem.reshape(b * N_MEM, d), p["g_mem"], p["w_mem_kv"], tm_pref=256)
    gg = p["g_group"]
    oa = _window_attn(proj3, p["attn_sink"], gg[:ATTN_WIDTH])
    osg = _sgu(proj3, p["w_spatial"], p["b_spatial"], p["g_sgu"], gg[ATTN_WIDTH:ATTN_WIDTH + SGU_WIDTH])
    ox = _mem_xattn(proj3, kv.reshape(b, N_MEM, 2 * XATTN_WIDTH), gg[ATTN_WIDTH + SGU_WIDTH:])
    mixed = (oa.reshape(t, -1), osg.reshape(t, -1), ox.reshape(t, -1))
    if w_down is None:
        return _out_proj(*mixed, p["w_out"], x2)
    e, f, dd = w_down.shape
    rb = _cast_rows(e * f, dd, (t // _tile(t, PROJ_TM)) * (D_MODEL // OUT_TN))
    if rb is None:
        return _out_proj(*mixed, p["w_out"], x2), wgu, w_down.astype(BF16)
    h, wd = _out_proj(*mixed, p["w_out"], x2, cast_src=w_down.reshape(e * f, dd), cast_rb=rb)
    return h, wgu, wd.reshape(e, f, dd)


def kernel(x_prompt, x_sample, mem_prompt, mem_sample, g_mix, w_in, attn_sink, g_sgu, w_spatial, b_spatial,
           g_mem, w_mem_kv, g_group, w_out, g_ffn, w_router, b_router, w_gate_up, b_gate_up, w_down, b_down,
           g_final):
    assert g_mix.shape[0] == 1
    q, k, v, u, vs, xq = jnp.split(w_in[0], [2048, 2304, 2560, 3584, 4608], axis=1)
    p = {
        "g_mix": g_mix[0],
        "w_in": jnp.concatenate([q, u, vs, xq, k, v], axis=1).astype(BF16),
        "attn_sink": attn_sink[0],
        "g_sgu": g_sgu[0],
        "w_spatial": w_spatial[0].astype(BF16),
        "b_spatial": jnp.repeat(b_spatial[0].T, SGU_GROUP_DIM, axis=1),
        "g_mem": g_mem[0],
        "w_mem_kv": w_mem_kv[0].astype(BF16),
        "g_group": g_group[0],
        "w_out": w_out[0].astype(BF16),
    }
    h1 = _mixer(x_prompt, mem_prompt, p)
    h2, wgu, wd = _mixer(x_sample, mem_sample, p, w_gate_up[0], w_down[0])
    t1, t2 = h1.shape[0], h2.shape[0]
    t_all = t1 + t2
    tm = 256
    assert t1 % tm == 0 and t2 % tm == 0
    xn, meta, gates, cnt = _router(h1, h2, g_ffn[0], w_router[0].astype(BF16), b_router[0], tm=tm)
    nt, te, tok, dstp = _route(meta, cnt, t_all, tm)
    spare0 = TOP_K * t_all
    y = _moe(xn, nt, te, tok, dstp, wgu, b_gate_up[0][:, None, :], wd, b_down[0][:, None, :],
             tm=tm, y_rows=spare0 + 2 * tm, spare0=spare0)
    out1 = _combine(h1, y, gates, g_final, t_all, 0, tm=tm).reshape(x_prompt.shape)
    out2 = _combine(h2, y, gates, g_final, t_all, t1, tm=tm).reshape(x_sample.shape)
    return out1, out2
```

```python
import functools

import jax
import jax.numpy as jnp
import numpy as np
from jax import lax
from jax.experimental import pallas as pl
from jax.experimental.pallas import tpu as pltpu

D_MODEL = 4096
N_MEM = 256
HEAD_DIM = 64
ATTN_WIDTH = 2048
ATTN_HEADS = 32
KV_HEADS = 4
KV_WIDTH = 256
WINDOW = 128
BLOCK = 128
ROPE_DIM = 16
ROPE_THETA = 500000.0
SGU_WIDTH = 1024
SGU_GROUPS = 4
SGU_GROUP_DIM = 256
CHUNK = 128
XATTN_WIDTH = 1024
XATTN_HEADS = 4
XATTN_HEAD_DIM = 256
IN_WIDTH = 5632
N_EXPERTS = 32
TOP_K = 4
D_EXPERT = 512
SWIGLU_LIMIT = 7.0
SWIGLU_ALPHA = 1.702
EPS = 1e-5
NEG_INF = -1e30

LANES = 128
COL_Q, COL_U, COL_VS, COL_XQ, COL_K, COL_V = 0, 2048, 3072, 4096, 5120, 5376

F32 = jnp.float32
BF16 = jnp.bfloat16
MIB = 1024 * 1024


def _params(semantics, vmem_mib):
    return pltpu.CompilerParams(dimension_semantics=semantics, vmem_limit_bytes=vmem_mib * MIB)


def _tile(n, pref):
    t = min(n, pref)
    while n % t or t % LANES:
        t -= LANES
    assert t > 0
    return t


def _rope_group(a, c, s1, s2):
    return a * c + pltpu.roll(a, LANES - ROPE_DIM // 2, 1) * s1 + pltpu.roll(a, ROPE_DIM // 2, 1) * s2


PROJ_TM, PROJ_TN, OUT_TN = 512, 512, 1024
CAST_BLOCK_BYTES = 2 * MIB


def _cast_rows(rows, cols, steps):
    nblk = 1 << (steps.bit_length() - 1)
    while rows % nblk:
        nblk //= 2
    rb = rows // nblk
    return rb if rb * cols * 4 <= CAST_BLOCK_BYTES and rb % 16 == 0 else None


def _cast_spec(nblk, rb, cols, steps_per_row):
    return pl.BlockSpec((rb, cols), lambda i, j: (jnp.minimum(i * steps_per_row + j, nblk - 1), 0))


def _norm_proj_kernel(*refs, rope, tn, cast):
    if cast:
        *ins, cast_in_ref, o_ref, cast_out_ref, xn_ref = refs
        refs = (*ins, o_ref, xn_ref)
        cast_out_ref[...] = cast_in_ref[...].astype(BF16)
    if rope:
        x_ref, g_ref, w_ref, c_ref, s1_ref, s2_ref, o_ref, xn_ref = refs
    else:
        x_ref, g_ref, w_ref, o_ref, xn_ref = refs
    j = pl.program_id(1)

    @pl.when(j == 0)
    def _():
        x = x_ref[...]
        ms = jnp.mean(x * x, axis=-1, keepdims=True)
        xn_ref[...] = (x * lax.rsqrt(ms + EPS) * g_ref[...]).astype(BF16)

    acc = jnp.dot(xn_ref[...], w_ref[...], preferred_element_type=F32)
    if not rope:
        o_ref[...] = acc.astype(o_ref.dtype)
        return

    ngroups = tn // LANES
    q_tiles = ATTN_WIDTH // tn
    k_tile = COL_K // tn
    k_groups = KV_WIDTH // LANES

    def store(n_rope):
        c, s1, s2 = c_ref[...], s1_ref[...], s2_ref[...]
        for gidx in range(ngroups):
            a = acc[:, gidx * LANES:(gidx + 1) * LANES]
            if gidx < n_rope:
                a = _rope_group(a, c, s1, s2)
            o_ref[:, gidx * LANES:(gidx + 1) * LANES] = a.astype(o_ref.dtype)

    @pl.when(j < q_tiles)
    def _():
        store(ngroups)

    @pl.when(j == k_tile)
    def _():
        store(k_groups)

    @pl.when(jnp.logical_and(j >= q_tiles, j != k_tile))
    def _():
        store(0)


def _norm_proj(x, g, w, rope_tabs=None, seq=None, *, tm_pref=PROJ_TM, tn=PROJ_TN, cast_src=None, cast_rb=None):
    t, d = x.shape
    n = w.shape[1]
    tm = _tile(t if seq is None else seq, tm_pref)
    rope = rope_tabs is not None
    cast = cast_src is not None
    in_specs = [
        pl.BlockSpec((tm, d), lambda i, j: (i, 0)),
        pl.BlockSpec((1, d), lambda i, j: (0, 0)),
        pl.BlockSpec((d, tn), lambda i, j: (0, j)),
    ]
    args = [x, g.reshape(1, d), w]
    if rope:
        assert COL_K % tn == 0 and ATTN_WIDTH % tn == 0 and tn >= KV_WIDTH
        sblocks = seq // tm
        tab_spec = pl.BlockSpec((tm, LANES), lambda i, j: (i % sblocks, 0))
        in_specs += [tab_spec, tab_spec, tab_spec]
        args += list(rope_tabs)
    out_shape = jax.ShapeDtypeStruct((t, n), BF16)
    out_specs = pl.BlockSpec((tm, tn), lambda i, j: (i, j))
    if cast:
        rows, cols = cast_src.shape
        nblk = rows // cast_rb
        assert rows % cast_rb == 0 and nblk <= (t // tm) * (n // tn)
        spec = _cast_spec(nblk, cast_rb, cols, n // tn)
        in_specs.append(spec)
        args.append(cast_src)
        out_shape = (out_shape, jax.ShapeDtypeStruct((rows, cols), BF16))
        out_specs = (out_specs, spec)
    return pl.pallas_call(
        functools.partial(_norm_proj_kernel, rope=rope, tn=tn, cast=cast),
        out_shape=out_shape,
        grid=(t // tm, n // tn),
        in_specs=in_specs,
        out_specs=out_specs,
        scratch_shapes=[pltpu.VMEM((tm, d), BF16)],
        compiler_params=_params(("arbitrary", "arbitrary") if cast else ("parallel", "arbitrary"), 48),
        name="norm_proj_rope" if rope else "norm_proj",
    )(*args)


def _rope_tables(seq):
    half = ROPE_DIM // 2
    inv_freq = ROPE_THETA ** (-jnp.arange(0, ROPE_DIM, 2, dtype=F32) / ROPE_DIM)
    ang = jnp.arange(seq, dtype=F32)[:, None] * inv_freq[None, :]
    cos, sin = jnp.cos(ang), jnp.sin(ang)
    dim = np.arange(LANES) % HEAD_DIM
    sel = dim % half
    cos_l, sin_l = cos[:, sel], sin[:, sel]
    c = jnp.where(dim < ROPE_DIM, cos_l, 1.0)
    s1 = jnp.where(dim < half, -sin_l, 0.0)
    s2 = jnp.where((dim >= half) & (dim < ROPE_DIM), sin_l, 0.0)
    return c.astype(F32), s1.astype(F32), s2.astype(F32)


def _window_attn_kernel(sink_ref, q_ref, kp_ref, kc_ref, kn_ref, vp_ref, vc_ref, vn_ref, gg_ref,
                        o_ref, obuf_ref, *, nb):
    n = pl.program_id(1)
    band = 3 * BLOCK
    qi = lax.broadcasted_iota(jnp.int32, (BLOCK, band), 0)
    ki = lax.broadcasted_iota(jnp.int32, (BLOCK, band), 1)
    rel = ki - BLOCK - qi
    lo = jnp.where(n == 0, BLOCK, 0)
    hi = jnp.where(n == nb - 1, 2 * BLOCK, band)
    valid = (jnp.abs(rel) <= WINDOW) & (ki >= lo) & (ki < hi)
    lane_k = lax.broadcasted_iota(jnp.int32, (band, LANES), 1)
    left_k = lane_k < HEAD_DIM
    lane_q = lax.broadcasted_iota(jnp.int32, (BLOCK, LANES), 1)
    left_q = lane_q < HEAD_DIM
    ones_l = jnp.where(left_k, 1.0, 0.0).astype(F32)
    ones_r = 1.0 - ones_l
    scale = HEAD_DIM ** -0.5
    ssq = jnp.zeros((BLOCK, 1), F32)

    for h in range(KV_HEADS):
        slab = (h // 2) * LANES
        k3 = jnp.concatenate([r[:, slab:slab + LANES] for r in (kp_ref, kc_ref, kn_ref)], axis=0).astype(F32)
        v3 = jnp.concatenate([r[:, slab:slab + LANES] for r in (vp_ref, vc_ref, vn_ref)], axis=0).astype(F32)
        k3r = pltpu.roll(k3, HEAD_DIM, 1)
        v3r = pltpu.roll(v3, HEAD_DIM, 1)
        if h % 2 == 0:
            ka, kb, va, vb = k3, k3r, v3, v3r
        else:
            ka, kb, va, vb = k3r, k3, v3r, v3
        kbd = jnp.concatenate([jnp.where(left_k, ka, 0.0), jnp.where(left_k, 0.0, kb)], axis=0).astype(BF16)
        vbd = jnp.concatenate(
            [jnp.concatenate([jnp.where(left_k, va, 0.0), ones_l], axis=1),
             jnp.concatenate([jnp.where(left_k, 0.0, vb), ones_r], axis=1)], axis=0).astype(BF16)
        for p in range(ATTN_HEADS // KV_HEADS // 2):
            g = h * 4 + p
            q2 = q_ref[:, g * LANES:(g + 1) * LANES]
            s2 = lax.dot_general(q2, kbd, (((1,), (1,)), ((), ())), preferred_element_type=F32)
            sa = jnp.where(valid, s2[:, :band] * scale, NEG_INF)
            sb = jnp.where(valid, s2[:, band:] * scale, NEG_INF)
            sink_a = sink_ref[2 * g]
            sink_b = sink_ref[2 * g + 1]
            ma = jnp.maximum(jnp.max(sa, axis=-1, keepdims=True), sink_a)
            mb = jnp.maximum(jnp.max(sb, axis=-1, keepdims=True), sink_b)
            p2 = jnp.concatenate([jnp.exp(sa - ma), jnp.exp(sb - mb)], axis=1).astype(BF16)
            o2 = jnp.dot(p2, vbd, preferred_element_type=F32)
            den = o2[:, LANES:] + jnp.where(left_q, jnp.exp(sink_a - ma), jnp.exp(sink_b - mb))
            out = o2[:, :LANES] / den
            ssq = ssq + jnp.sum(out * out, axis=-1, keepdims=True)
            obuf_ref[:, g * LANES:(g + 1) * LANES] = out

    r = lax.rsqrt(ssq * (1.0 / ATTN_WIDTH) + EPS)
    o_ref[...] = (obuf_ref[...] * r * gg_ref[...]).astype(o_ref.dtype)


def _window_attn(proj3, sink, gg):
    b, s, _ = proj3.shape
    nb = s // BLOCK
    kcol, vcol = COL_K // KV_WIDTH, COL_V // KV_WIDTH

    def band_spec(col, shift):
        return pl.BlockSpec((None, BLOCK, KV_WIDTH),
                            lambda bi, n: (bi, jnp.clip(n + shift, 0, nb - 1), col))

    return pl.pallas_call(
        functools.partial(_window_attn_kernel, nb=nb),
        out_shape=jax.ShapeDtypeStruct((b, s, ATTN_WIDTH), BF16),
        grid=(b, nb),
        in_specs=[
            pl.BlockSpec(memory_space=pltpu.SMEM),
            pl.BlockSpec((None, BLOCK, ATTN_WIDTH), lambda bi, n: (bi, n, 0)),
            band_spec(kcol, -1), band_spec(kcol, 0), band_spec(kcol, 1),
            band_spec(vcol, -1), band_spec(vcol, 0), band_spec(vcol, 1),
            pl.BlockSpec((1, ATTN_WIDTH), lambda bi, n: (0, 0)),
        ],
        out_specs=pl.BlockSpec((None, BLOCK, ATTN_WIDTH), lambda bi, n: (bi, n, 0)),
        scratch_shapes=[pltpu.VMEM((BLOCK, ATTN_WIDTH), F32)],
        compiler_params=_params(("parallel", "parallel"), 32),
        name="window_attn",
    )(sink, proj3, proj3, proj3, proj3, proj3, proj3, proj3, gg.reshape(1, ATTN_WIDTH))


def _sgu_kernel(u_ref, v_ref, ws_ref, bs_ref, gs_ref, gg_ref, o_ref, *, chunks):
    for c in range(chunks):
        rows = slice(c * CHUNK, (c + 1) * CHUNK)
        ug = jax.nn.gelu(u_ref[rows, :].astype(F32))
        vg = jax.nn.gelu(v_ref[rows, :].astype(F32))
        mixed = []
        for h in range(SGU_GROUPS):
            cols = slice(h * SGU_GROUP_DIM, (h + 1) * SGU_GROUP_DIM)
            vh = vg[:, cols]
            r = lax.rsqrt(jnp.mean(vh * vh, axis=-1, keepdims=True) + EPS)
            vn = (vh * r * gs_ref[:, cols]).astype(BF16)
            mixed.append(jnp.dot(ws_ref[h], vn, preferred_element_type=F32))
        o = ug * (jnp.concatenate(mixed, axis=1) + bs_ref[...])
        r = lax.rsqrt(jnp.mean(o * o, axis=-1, keepdims=True) + EPS)
        o_ref[rows, :] = (o * r * gg_ref[...]).astype(o_ref.dtype)


def _sgu(proj3, ws, bs_full, gs, gg, *, rows_pref=512):
    b, s, _ = proj3.shape
    tr = _tile(s, rows_pref)
    const = lambda bi, n: (0, 0)
    return pl.pallas_call(
        functools.partial(_sgu_kernel, chunks=tr // CHUNK),
        out_shape=jax.ShapeDtypeStruct((b, s, SGU_WIDTH), BF16),
        grid=(b, s // tr),
        in_specs=[
            pl.BlockSpec((None, tr, SGU_WIDTH), lambda bi, n: (bi, n, COL_U // SGU_WIDTH)),
            pl.BlockSpec((None, tr, SGU_WIDTH), lambda bi, n: (bi, n, COL_VS // SGU_WIDTH)),
            pl.BlockSpec((SGU_GROUPS, CHUNK, CHUNK), lambda bi, n: (0, 0, 0)),
            pl.BlockSpec((CHUNK, SGU_WIDTH), const),
            pl.BlockSpec((1, SGU_WIDTH), const),
            pl.BlockSpec((1, SGU_WIDTH), const),
        ],
        out_specs=pl.BlockSpec((None, tr, SGU_WIDTH), lambda bi, n: (bi, n, 0)),
        compiler_params=_params(("parallel", "parallel"), 32),
        name="sgu",
    )(proj3, proj3, ws, bs_full, gs.reshape(1, SGU_WIDTH), gg.reshape(1, SGU_WIDTH))


def _mem_xattn_kernel(q_ref, kv_ref, gg_ref, o_ref):
    scale = XATTN_HEAD_DIM ** -0.5
    outs = []
    ssq = None
    for h in range(XATTN_HEADS):
        cols = slice(h * XATTN_HEAD_DIM, (h + 1) * XATTN_HEAD_DIM)
        vcols = slice(XATTN_WIDTH + h * XATTN_HEAD_DIM, XATTN_WIDTH + (h + 1) * XATTN_HEAD_DIM)
        s = lax.dot_general(q_ref[:, cols], kv_ref[:, cols], (((1,), (1,)), ((), ())),
                            preferred_element_type=F32) * scale
        m = jnp.max(s, axis=-1, keepdims=True)
        p = jnp.exp(s - m)
        l = jnp.sum(p, axis=-1, keepdims=True)
        o = jnp.dot(p.astype(BF16), kv_ref[:, vcols], preferred_element_type=F32) / l
        sq = jnp.sum(o * o, axis=-1, keepdims=True)
        ssq = sq if ssq is None else ssq + sq
        outs.append(o)
    r = lax.rsqrt(ssq * (1.0 / XATTN_WIDTH) + EPS)
    for h in range(XATTN_HEADS):
        cols = slice(h * XATTN_HEAD_DIM, (h + 1) * XATTN_HEAD_DIM)
        o_ref[:, cols] = (outs[h] * r * gg_ref[:, cols]).astype(o_ref.dtype)


def _mem_xattn(proj3, kv3, gg, *, rows_pref=512):
    b, s, _ = proj3.shape
    tq = _tile(s, rows_pref)
    return pl.pallas_call(
        _mem_xattn_kernel,
        out_shape=jax.ShapeDtypeStruct((b, s, XATTN_WIDTH), BF16),
        grid=(b, s // tq),
        in_specs=[
            pl.BlockSpec((None, tq, XATTN_WIDTH), lambda bi, n: (bi, n, COL_XQ // XATTN_WIDTH)),
            pl.BlockSpec((None, N_MEM, 2 * XATTN_WIDTH), lambda bi, n: (bi, 0, 0)),
            pl.BlockSpec((1, XATTN_WIDTH), lambda bi, n: (0, 0)),
        ],
        out_specs=pl.BlockSpec((None, tq, XATTN_WIDTH), lambda bi, n: (bi, n, 0)),
        compiler_params=_params(("parallel", "parallel"), 32),
        name="mem_xattn",
    )(proj3, kv3, gg.reshape(1, XATTN_WIDTH))


def _out_proj_kernel(oa_ref, os_ref, ox_ref, w_ref, x_ref, *rest):
    if len(rest) == 3:
        cast_in_ref, h_ref, cast_out_ref = rest
        cast_out_ref[...] = cast_in_ref[...].astype(BF16)
    else:
        (h_ref,) = rest
    a1, a2 = ATTN_WIDTH, ATTN_WIDTH + SGU_WIDTH
    acc = jnp.dot(oa_ref[...], w_ref[:a1, :], preferred_element_type=F32)
    acc += jnp.dot(os_ref[...], w_ref[a1:a2, :], preferred_element_type=F32)
    acc += jnp.dot(ox_ref[...], w_ref[a2:, :], preferred_element_type=F32)
    h_ref[...] = x_ref[...] + acc


def _out_proj(oa, osg, ox, w, x, *, tm_pref=PROJ_TM, tn=OUT_TN, cast_src=None, cast_rb=None):
    t, d = x.shape
    tm = _tile(t, tm_pref)
    cast = cast_src is not None
    in_specs = [
        pl.BlockSpec((tm, ATTN_WIDTH), lambda i, j: (i, 0)),
        pl.BlockSpec((tm, SGU_WIDTH), lambda i, j: (i, 0)),
        pl.BlockSpec((tm, XATTN_WIDTH), lambda i, j: (i, 0)),
        pl.BlockSpec((d, tn), lambda i, j: (0, j)),
        pl.BlockSpec((tm, tn), lambda i, j: (i, j)),
    ]
    args = [oa, osg, ox, w, x]
    out_shape = jax.ShapeDtypeStruct((t, d), F32)
    out_specs = pl.BlockSpec((tm, tn), lambda i, j: (i, j))
    if cast:
        rows, cols = cast_src.shape
        nblk = rows // cast_rb
        assert rows % cast_rb == 0 and nblk <= (t // tm) * (d // tn)
        spec = _cast_spec(nblk, cast_rb, cols, d // tn)
        in_specs.append(spec)
        args.append(cast_src)
        out_shape = (out_shape, jax.ShapeDtypeStruct((rows, cols), BF16))
        out_specs = (out_specs, spec)
    return pl.pallas_call(
        _out_proj_kernel,
        out_shape=out_shape,
        grid=(t // tm, d // tn),
        in_specs=in_specs,
        out_specs=out_specs,
        compiler_params=_params(("arbitrary", "arbitrary") if cast else ("parallel", "arbitrary"), 48),
        name="out_proj",
    )(*args)


SLAB_ROWS = D_MODEL // LANES
SLAB_PITCH = SLAB_ROWS + 4


def _router_kernel(h1_ref, h2_ref, g_ref, wr_ref, br_ref, xn_ref, meta_ref, gate_ref, cnt_ref, run_ref,
                   *, nb1, tm):
    i = pl.program_id(0)

    @pl.when(i == 0)
    def _():
        run_ref[...] = jnp.zeros_like(run_ref)

    def body(h_ref):
        x = h_ref[...]
        ms = jnp.mean(x * x, axis=-1, keepdims=True)
        xn = x * lax.rsqrt(ms + EPS) * g_ref[...]
        for c in range(SLAB_ROWS):
            xn_ref[pl.ds(c, tm, stride=SLAB_PITCH), :] = xn[:, c * LANES:(c + 1) * LANES]
        for c in range(SLAB_ROWS, SLAB_PITCH):
            xn_ref[pl.ds(c, tm, stride=SLAB_PITCH), :] = jnp.zeros((tm, LANES), F32)
        logits = jnp.dot(xn.astype(BF16), wr_ref[...], preferred_element_type=F32) + br_ref[...]
        lane_e = lax.broadcasted_iota(jnp.int32, (tm, N_EXPERTS), 1)
        lane_o = lax.broadcasted_iota(jnp.int32, (tm, LANES), 1)
        row = lax.broadcasted_iota(jnp.int32, (tm, tm), 0)
        col = lax.broadcasted_iota(jnp.int32, (tm, tm), 1)
        earlier = jnp.where(col < row, 1.0, 0.0).astype(BF16)
        vals, idxs = [], []
        l = logits
        for _ in range(TOP_K):
            m = jnp.max(l, axis=-1, keepdims=True)
            e = jnp.min(jnp.where(l == m, lane_e, N_EXPERTS), axis=-1, keepdims=True)
            vals.append(m)
            idxs.append(e)
            l = jnp.where(lane_e == e, -jnp.inf, l)
        es = [jnp.exp(v - vals[0]) for v in vals]
        den = es[0] + es[1] + es[2] + es[3]
        base = run_ref[...]
        meta = jnp.zeros((tm, LANES), jnp.int32)
        gate_out = jnp.zeros((tm, LANES), F32)
        for k in range(TOP_K):
            onehot = jnp.where(lane_e == idxs[k], 1.0, 0.0)
            before = jnp.dot(earlier, onehot.astype(BF16), preferred_element_type=F32)
            rank = jnp.sum(onehot * (before + base), axis=-1, keepdims=True)
            base = base + jnp.sum(onehot, axis=0, keepdims=True)
            meta = jnp.where(lane_o == k, idxs[k], meta)
            meta = jnp.where(lane_o == TOP_K + k, rank.astype(jnp.int32), meta)
            gate_out = jnp.where(lane_o == k, es[k] / den, gate_out)
        run_ref[...] = base
        cnt_ref[...] = base
        meta_ref[...] = meta
        gate_ref[...] = gate_out

    @pl.when(i < nb1)
    def _():
        body(h1_ref)

    @pl.when(i >= nb1)
    def _():
        body(h2_ref)


def _router(h1, h2, g, wr, br, *, tm):
    t1, d = h1.shape
    t2 = h2.shape[0]
    nb1, nb2 = t1 // tm, t2 // tm
    t_all = t1 + t2
    const = lambda i: (0, 0)
    return pl.pallas_call(
        functools.partial(_router_kernel, nb1=nb1, tm=tm),
        out_shape=(jax.ShapeDtypeStruct((t_all * SLAB_PITCH, LANES), F32),
                   jax.ShapeDtypeStruct((t_all, LANES), jnp.int32),
                   jax.ShapeDtypeStruct((t_all, LANES), F32),
                   jax.ShapeDtypeStruct((1, N_EXPERTS), F32)),
        grid=(nb1 + nb2,),
        in_specs=[
            pl.BlockSpec((tm, d), lambda i: (jnp.minimum(i, nb1 - 1), 0)),
            pl.BlockSpec((tm, d), lambda i: (jnp.maximum(i - nb1, 0), 0)),
            pl.BlockSpec((1, d), const),
            pl.BlockSpec((d, N_EXPERTS), const),
            pl.BlockSpec((1, N_EXPERTS), const),
        ],
        out_specs=(pl.BlockSpec((tm * SLAB_PITCH, LANES), lambda i: (i, 0)),
                   pl.BlockSpec((tm, LANES), lambda i: (i, 0)),
                   pl.BlockSpec((tm, LANES), lambda i: (i, 0)),
                   pl.BlockSpec((1, N_EXPERTS), const)),
        scratch_shapes=[pltpu.VMEM((1, N_EXPERTS), F32)],
        compiler_params=_params(("arbitrary",), 48),
        name="router",
    )(h1, h2, g.reshape(1, d), wr, br.reshape(1, N_EXPERTS))


def _moe_kernel(nt_ref, te_ref, tok_ref, tokn_ref, dstp_ref, xn_hbm, wgu_ref, bgu_ref, wd_ref, bd_ref,
                y_hbm, xs0, xs1, yb0, yb1, xb, gsem, ssem, *, tm, spare0):
    del te_ref
    i = pl.program_id(0)
    nt = nt_ref[0]
    rows = range(tm)

    def gather(idx_ref, r, xs, s):
        return pltpu.make_async_copy(xn_hbm.at[pl.ds(idx_ref[0, 0, r], SLAB_ROWS), :],
                                     xs.at[pl.ds(r * SLAB_PITCH, SLAB_ROWS), :], gsem.at[s])

    def scatter(r, yb, s):
        return pltpu.make_async_copy(yb.at[pl.ds(r, 1), :], y_hbm.at[pl.ds(dstp_ref[0, 0, r], 1), :], ssem.at[s])

    def step(s, xs, xs_next, yb, yb_prev):
        for r in rows:
            gather(tok_ref, r, xs, s).wait()

        @pl.when(i >= 1)
        def _():
            for r in rows:
                scatter(r, yb, s).wait()

        @pl.when(nt > 0)
        def _():
            for r in rows:
                gather(tokn_ref, r, xs_next, 1 - s).start()
                scatter(r, yb_prev, 1 - s).start()
        for c in range(SLAB_ROWS):
            xb[:, c * LANES:(c + 1) * LANES] = xs[pl.ds(c, tm, stride=SLAB_PITCH), :].astype(BF16)
        gu = jnp.dot(xb[...], wgu_ref[...], preferred_element_type=F32) + bgu_ref[...]
        gate = jnp.minimum(gu[:, :D_EXPERT], SWIGLU_LIMIT)
        up = jnp.clip(gu[:, D_EXPERT:], -SWIGLU_LIMIT, SWIGLU_LIMIT)
        hid = (up + 1.0) * (gate * jax.nn.sigmoid(SWIGLU_ALPHA * gate))
        yb[...] = jnp.dot(hid.astype(BF16), wd_ref[...], preferred_element_type=F32) + bd_ref[...]

    @pl.when(i == 0)
    def _():
        yb1[...] = jnp.zeros_like(yb1)
        for r in rows:
            gather(tok_ref, r, xs0, 0).start()

    @pl.when(jnp.logical_and(i < nt, i % 2 == 0))
    def _():
        step(0, xs0, xs1, yb0, yb1)

    @pl.when(jnp.logical_and(i < nt, i % 2 == 1))
    def _():
        step(1, xs1, xs0, yb1, yb0)

    def drain(s, xs, yb, yb_prev):
        for r in rows:
            gather(tok_ref, r, xs, s).wait()
        for r in rows:
            scatter(r, yb, s).wait()
        for r in rows:
            scatter(r, yb_prev, 1 - s).start()
        for r in rows:
            scatter(r, yb_prev, 1 - s).wait()
        for q, ybq in enumerate((yb0, yb1)):
            fill = pltpu.make_async_copy(ybq, y_hbm.at[pl.ds(spare0 + q * tm, tm), :], ssem.at[q])
            fill.start()
            fill.wait()

    @pl.when(jnp.logical_and(i == nt, i % 2 == 0))
    def _():
        drain(0, xs0, yb0, yb1)

    @pl.when(jnp.logical_and(i == nt, i % 2 == 1))
    def _():
        drain(1, xs1, yb1, yb0)


def _moe(xn, nt, te, tok, dstp, wgu, bgu, wd, bd, *, tm, y_rows, spare0):
    ntmax = te.shape[0]
    d = D_MODEL
    row_spec = lambda shift, nrows: pl.BlockSpec(
        (1, 1, tm), lambda i, nt_, te_: (jnp.minimum(i + shift, nrows - 1), 0, 0), memory_space=pltpu.SMEM)
    expert = lambda i, nt_, te_: (te_[jnp.minimum(i, ntmax - 1)], 0, 0)
    grid_spec = pltpu.PrefetchScalarGridSpec(
        num_scalar_prefetch=2,
        grid=(ntmax + 1,),
        in_specs=[
            row_spec(0, ntmax), row_spec(1, ntmax), row_spec(0, ntmax + 1),
            pl.BlockSpec(memory_space=pl.ANY),
            pl.BlockSpec((None, d, 2 * D_EXPERT), expert),
            pl.BlockSpec((None, 1, 2 * D_EXPERT), expert),
            pl.BlockSpec((None, D_EXPERT, d), expert),
            pl.BlockSpec((None, 1, d), expert),
        ],
        out_specs=pl.BlockSpec(memory_space=pl.ANY),
        scratch_shapes=[
            pltpu.VMEM((tm * SLAB_PITCH, LANES), F32),
            pltpu.VMEM((tm * SLAB_PITCH, LANES), F32),
            pltpu.VMEM((tm, d), F32),
            pltpu.VMEM((tm, d), F32),
            pltpu.VMEM((tm, d), BF16),
            pltpu.SemaphoreType.DMA((2,)),
            pltpu.SemaphoreType.DMA((2,)),
        ],
    )
    return pl.pallas_call(
        functools.partial(_moe_kernel, tm=tm, spare0=spare0),
        out_shape=jax.ShapeDtypeStruct((y_rows, d), F32),
        grid_spec=grid_spec,
        compiler_params=_params(("arbitrary",), 58),
        name="moe_experts",
    )(nt, te, tok, tok, dstp, xn, wgu, bgu, wd, bd)


INVERT_CHUNK = 8192


def _invert_kernel(slot_ref, fill_hbm, table_ref, sem, *, chunk, nslot):
    i = pl.program_id(0)

    @pl.when(i == 0)
    def _():
        fill = pltpu.make_async_copy(fill_hbm, table_ref, sem)
        fill.start()
        fill.wait()

    def body(a, c):
        table_ref[slot_ref[0, 0, a]] = i * chunk + a
        return c
    lax.fori_loop(0, chunk, body, 0, unroll=8)


def _invert(slot, nslot):
    n = slot.shape[0]
    chunk = min(INVERT_CHUNK, n)
    assert n % chunk == 0
    return pl.pallas_call(
        functools.partial(_invert_kernel, chunk=chunk, nslot=nslot),
        out_shape=jax.ShapeDtypeStruct((nslot,), jnp.int32),
        grid=(n // chunk,),
        in_specs=[pl.BlockSpec((1, 1, chunk), lambda i: (i, 0, 0), memory_space=pltpu.SMEM),
                  pl.BlockSpec(memory_space=pl.ANY)],
        out_specs=pl.BlockSpec(memory_space=pltpu.SMEM),
        scratch_shapes=[pltpu.SemaphoreType.DMA],
        compiler_params=pltpu.CompilerParams(dimension_semantics=("arbitrary",)),
        name="invert_slots",
    )(slot.reshape(n // chunk, 1, chunk), jnp.full((nslot,), -1, jnp.int32))


def _route(meta, cnt, t_all, tm):
    a_all = TOP_K * t_all
    ntmax = (a_all + N_EXPERTS * (tm - 1)) // tm
    cnt = cnt.reshape(N_EXPERTS).astype(jnp.int32)
    tiles_e = (cnt + tm - 1) // tm
    tile_end = jnp.cumsum(tiles_e)
    tile_start = tile_end - tiles_e
    nt = tile_end[-1:]
    ti = jnp.arange(ntmax, dtype=jnp.int32)
    te = jnp.minimum(jnp.sum((tile_end[None, :] <= ti[:, None]).astype(jnp.int32), axis=1), N_EXPERTS - 1)
    idx, rank = meta[:, :TOP_K], meta[:, TOP_K:2 * TOP_K]
    slot = tile_start[idx] * tm + rank
    table = _invert(slot.reshape(-1), ntmax * tm).reshape(ntmax, tm)
    valid = table >= 0
    tok = jnp.where(valid, table // TOP_K, 0)
    tokp = tok * SLAB_PITCH
    r = jnp.arange(tm, dtype=jnp.int32)[None, :]
    spare = a_all + (ti % 2)[:, None] * tm + r
    dst = jnp.where(valid, (table % TOP_K) * t_all + tok, spare)
    dstp = jnp.concatenate([a_all + tm + r, dst], axis=0)
    return nt.astype(jnp.int32), te, tokp.reshape(ntmax, 1, tm), dstp.reshape(ntmax + 1, 1, tm)


def _combine_kernel(h_ref, y0_ref, y1_ref, y2_ref, y3_ref, gate_ref, g_ref, o_ref):
    acc = h_ref[...]
    gates = gate_ref[...]
    for k, y_ref in enumerate((y0_ref, y1_ref, y2_ref, y3_ref)):
        acc = acc + gates[:, k:k + 1] * y_ref[...]
    ms = jnp.mean(acc * acc, axis=-1, keepdims=True)
    o_ref[...] = acc * lax.rsqrt(ms + EPS) * g_ref[...]


def _combine(h, y, gates, g, t_all, t0, *, tm):
    t, d = h.shape
    yspec = lambda k: pl.BlockSpec((tm, d), lambda i: ((k * t_all + t0) // tm + i, 0))
    return pl.pallas_call(
        _combine_kernel,
        out_shape=jax.ShapeDtypeStruct((t, d), F32),
        grid=(t // tm,),
        in_specs=[pl.BlockSpec((tm, d), lambda i: (i, 0)),
                  yspec(0), yspec(1), yspec(2), yspec(3),
                  pl.BlockSpec((tm, LANES), lambda i: (t0 // tm + i, 0)),
                  pl.BlockSpec((1, d), lambda i: (0, 0))],
        out_specs=pl.BlockSpec((tm, d), lambda i: (i, 0)),
        compiler_params=_params(("parallel",), 56),
        name="combine",
    )(h, y, y, y, y, gates, g.reshape(1, d))


def _mixer(x, mem, p, w_gate_up=None, w_down=None):
    b, s, d = x.shape
    t = b * s
    x2 = x.reshape(t, d)
    tabs = _rope_tables(s)
    wgu = wd = None
    if w_gate_up is not None:
        e, dd, f2 = w_gate_up.shape
        rb = _cast_rows(e * dd, f2, (t // _tile(s, PROJ_TM)) * (IN_WIDTH // PROJ_TN))
        if rb is None:
            wgu = w_gate_up.astype(BF16)
    if w_gate_up is None or wgu is not None:
        proj = _norm_proj(x2, p["g_mix"], p["w_in"], tabs, s)
    else:
        proj, wgu = _norm_proj(x2, p["g_mix"], p["w_in"], tabs, s, cast_src=w_gate_up.reshape(e * dd, f2), cast_rb=rb)
        wgu = wgu.reshape(e, dd, f2)
    proj3 = proj.reshape(b, s, IN_WIDTH)
    kv = _norm_proj(mem.reshape(b * N_MEM, d), p["g_mem"], p["w_mem_kv"], tm_pref=256)
    gg = p["g_group"]
    oa = _window_attn(proj3, p["attn_sink"], gg[:ATTN_WIDTH])
    osg = _sgu(proj3, p["w_spatial"], p["b_spatial"], p["g_sgu"], gg[ATTN_WIDTH:ATTN_WIDTH + SGU_WIDTH])
    ox = _mem_xattn(proj3, kv.reshape(b, N_MEM, 2 * XATTN_WIDTH), gg[ATTN_WIDTH + SGU_WIDTH:])
    mixed = (oa.reshape(t, -1), osg.reshape(t, -1), ox.reshape(t, -1))
    if w_down is None:
        return _out_proj(*mixed, p["w_out"], x2)
    e, f, dd = w_down.shape
    rb = _cast_rows(e * f, dd, (t // _tile(t, PROJ_TM)) * (D_MODEL // OUT_TN))
    if rb is None:
        return _out_proj(*mixed, p["w_out"], x2), wgu, w_down.astype(BF16)
    h, wd = _out_proj(*mixed, p["w_out"], x2, cast_src=w_down.reshape(e * f, dd), cast_rb=rb)
    return h, wgu, wd.reshape(e, f, dd)


def kernel(x_prompt, x_sample, mem_prompt, mem_sample, g_mix, w_in, attn_sink, g_sgu, w_spatial, b_spatial,
           g_mem, w_mem_kv, g_group, w_out, g_ffn, w_router, b_router, w_gate_up, b_gate_up, w_down, b_down,
           g_final):
    assert g_mix.shape[0] == 1
    q, k, v, u, vs, xq = jnp.split(w_in[0], [2048, 2304, 2560, 3584, 4608], axis=1)
    p = {
        "g_mix": g_mix[0],
        "w_in": jnp.concatenate([q, u, vs, xq, k, v], axis=1).astype(BF16),
        "attn_sink": attn_sink[0],
        "g_sgu": g_sgu[0],
        "w_spatial": w_spatial[0].astype(BF16),
        "b_spatial": jnp.repeat(b_spatial[0].T, SGU_GROUP_DIM, axis=1),
        "g_mem": g_mem[0],
        "w_mem_kv": w_mem_kv[0].astype(BF16),
        "g_group": g_group[0],
        "w_out": w_out[0].astype(BF16),
    }
    h1 = _mixer(x_prompt, mem_prompt, p)
    h2, wgu, wd = _mixer(x_sample, mem_sample, p, w_gate_up[0], w_down[0])
    t1, t2 = h1.shape[0], h2.shape[0]
    t_all = t1 + t2
    tm = 256
    assert t1 % tm == 0 and t2 % tm == 0
    xn, meta, gates, cnt = _router(h1, h2, g_ffn[0], w_router[0].astype(BF16), b_router[0], tm=tm)
    nt, te, tok, dstp = _route(meta, cnt, t_all, tm)
    spare0 = TOP_K * t_all
    y = _moe(xn, nt, te, tok, dstp, wgu, b_gate_up[0][:, None, :], wd, b_down[0][:, None, :],
             tm=tm, y_rows=spare0 + 2 * tm, spare0=spare0)
    out1 = _combine(h1, y, gates, g_final, t_all, 0, tm=tm).reshape(x_prompt.shape)
    out2 = _combine(h2, y, gates, g_final, t_all, t1, tm=tm).reshape(x_sample.shape)
    return out1, out2
```
